```python
import math
import jax, jax.numpy as jnp
from jax import lax
import numpy as np

D_MODEL = 1024
BATCH = 8
SEQ = 2048
DEPTH = 1
DEC_BATCH = 32
DEC_SEQ = 16
PAST_LEN = 1024

CHUNK = 64
HEAD_DIM = 64
N_Q_HEADS = (D_MODEL // 2) // HEAD_DIM
N_KV_HEADS = 2
GQA_GROUP = N_Q_HEADS // N_KV_HEADS
WINDOW = 128
WIN_CHUNKS = WINDOW // CHUNK
ATTN_WIDTH = N_Q_HEADS * HEAD_DIM
KV_WIDTH = N_KV_HEADS * HEAD_DIM
POOL_WINDOWS = (2, 4, 8, 16)
N_POOL_GROUPS = len(POOL_WINDOWS)
POOL_WIDTH = D_MODEL // 2
POOL_GROUP_WIDTH = POOL_WIDTH // N_POOL_GROUPS
POOL_PAD = max(POOL_WINDOWS) - 1
MIX_WIDTH = ATTN_WIDTH + POOL_WIDTH
IN_WIDTH = ATTN_WIDTH + 2 * KV_WIDTH + POOL_WIDTH
D_FF = ((8 * D_MODEL + 3 * 256 - 1) // (3 * 256)) * 256
PLE_DIM = 256
N_BUCKETS = 32
MAX_DISTANCE = 128
RMS_EPS = 1e-6
MASK_VALUE = -1e30

kernel_name = "hybrid_swa_sink_pool_stream_step"


def _rmsnorm(x, g):
    xf = x.astype(jnp.float32)
    y = xf * lax.rsqrt(jnp.mean(xf * xf, axis=-1, keepdims=True) + RMS_EPS)
    return (y * g.astype(jnp.float32)).astype(x.dtype)


def _t5_bucket(rel):
    half = N_BUCKETS // 2
    max_exact = half // 2
    ret = jnp.where(rel > 0, half, 0)
    n = jnp.abs(rel)
    nf = jnp.maximum(n, 1).astype(jnp.float32)
    large = max_exact + (jnp.log(nf / max_exact) / math.log(MAX_DISTANCE / max_exact)
                         * (half - max_exact)).astype(jnp.int32)
    large = jnp.minimum(large, half - 1)
    return ret + jnp.where(n < max_exact, n, large)


def _rel_bias(q_pos, k_pos, table):
    b = table[_t5_bucket(k_pos[None, :] - q_pos[:, None])].astype(jnp.float32)
    b = jnp.transpose(b, (2, 0, 1))
    return b.reshape(N_KV_HEADS, GQA_GROUP, q_pos.shape[0], k_pos.shape[0])


def _sink_attention(q, k, v, bias, valid, sinks):
    s = jnp.einsum('bnqkgd,bnskd->bnkgqs', q.astype(jnp.float32), k.astype(jnp.float32))
    s = s * (HEAD_DIM ** -0.5) + bias[None, None]
    s = jnp.where(valid[None, :, None, None, None, :], s, MASK_VALUE)
    sk = sinks.astype(jnp.float32).reshape(N_KV_HEADS, GQA_GROUP)[None, None, :, :, None, None]
    m = jnp.maximum(jnp.max(s, axis=-1, keepdims=True), sk)
    p = jnp.exp(s - m)
    w = p / (jnp.sum(p, axis=-1, keepdims=True) + jnp.exp(sk - m))
    return jnp.einsum('bnkgqs,bnskd->bnqkgd', w.astype(v.dtype), v)


def _prompt_attention(q, k, v, table, sinks):
    B, S, _ = q.shape
    nc = S // CHUNK
    nk = (WIN_CHUNKS + 1) * CHUNK
    qb = q.reshape(B, nc, CHUNK, N_KV_HEADS, GQA_GROUP, HEAD_DIM)
    pad = ((0, 0), (WIN_CHUNKS, 0), (0, 0), (0, 0), (0, 0))
    kp = jnp.pad(k.reshape(B, nc, CHUNK, N_KV_HEADS, HEAD_DIM), pad)
    vp = jnp.pad(v.reshape(B, nc, CHUNK, N_KV_HEADS, HEAD_DIM), pad)
    kb = jnp.concatenate([kp[:, w:w + nc] for w in range(WIN_CHUNKS + 1)], axis=2)
    vb = jnp.concatenate([vp[:, w:w + nc] for w in range(WIN_CHUNKS + 1)], axis=2)
    q_pos = jnp.arange(CHUNK, dtype=jnp.int32)
    k_pos = jnp.arange(nk, dtype=jnp.int32) - WIN_CHUNKS * CHUNK
    bias = _rel_bias(q_pos, k_pos, table)
    valid = (jnp.arange(nc, dtype=jnp.int32)[:, None] * CHUNK + k_pos[None, :]) >= 0
    o = _sink_attention(qb, kb, vb, bias, valid, sinks)
    return o.reshape(B, S, ATTN_WIDTH)


def _sample_attention(q, k, v, cache_k, cache_v, table, sinks):
    B, L, _ = q.shape
    n_past = cache_k.shape[1]
    qb = q.reshape(B, 1, L, N_KV_HEADS, GQA_GROUP, HEAD_DIM)
    kb = jnp.concatenate([cache_k, k.reshape(B, L, N_KV_HEADS, HEAD_DIM)], axis=1)[:, None]
    vb = jnp.concatenate([cache_v, v.reshape(B, L, N_KV_HEADS, HEAD_DIM)], axis=1)[:, None]
    q_pos = PAST_LEN + jnp.arange(L, dtype=jnp.int32)
    k_pos = PAST_LEN - n_past + jnp.arange(n_past + L, dtype=jnp.int32)
    bias = _rel_bias(q_pos, k_pos, table)
    valid = (k_pos >= 0)[None, :]
    o = _sink_attention(qb, kb, vb, bias, valid, sinks)
    return o.reshape(B, L, ATTN_WIDTH)


def _multiscale_pool(ext, pos, w_pool, scale):
    L = pos.shape[0]
    cs = jnp.pad(jnp.cumsum(ext.astype(jnp.float32), axis=1), ((0, 0), (1, 0), (0, 0)))
    u = ext[:, POOL_PAD:].astype(jnp.float32)
    outs = []
    for g, w in enumerate(POOL_WINDOWS):
        sl = slice(g * POOL_GROUP_WIDTH, (g + 1) * POOL_GROUP_WIDTH)
        total = cs[:, POOL_PAD + 1:POOL_PAD + 1 + L, sl] - cs[:, POOL_PAD + 1 - w:POOL_PAD + 1 - w + L, sl]
        count = jnp.minimum(pos + 1, w).astype(jnp.float32)[None, :, None]
        diff = (total / count - u[..., sl]).astype(ext.dtype)
        outs.append(jnp.einsum('bld,de->ble', diff, w_pool[g]))
    return jnp.concatenate(outs, axis=-1) * scale


def _split_proj(x, g_mix_pre, w_in):
    z = _rmsnorm(x, g_mix_pre) @ w_in
    q, k, v, u = jnp.split(z, [ATTN_WIDTH, ATTN_WIDTH + KV_WIDTH, ATTN_WIDTH + 2 * KV_WIDTH], axis=-1)
    return q, k, v, u


def _finish_layer(x, attn, pool, pe, w_out, g_mix_post, g_ffn_pre, w_ffn_gate, w_ffn_up,
                  w_ffn_down, g_ffn_post, w_ple, w_ple_gate):
    mix = jnp.concatenate([attn, pool.astype(attn.dtype)], axis=-1) @ w_out
    x = x + _rmsnorm(mix, g_mix_post)
    h = _rmsnorm(x, g_ffn_pre)
    f = (jax.nn.silu(h @ w_ffn_gate) * (h @ w_ffn_up)) @ w_ffn_down
    x = x + _rmsnorm(f, g_ffn_post)
    return x + jax.nn.sigmoid(x @ w_ple_gate) * (pe @ w_ple)


def setup_inputs(seed: int = 0) -> dict:
    key = jax.random.key(seed)
    ks = jax.random.split(key, 24)
    nrm = lambda k, shape, s=1.0: jax.random.normal(k, shape, jnp.float32) * s
    win_rows = min(WINDOW, PAST_LEN)
    return {
        "x_prompt": nrm(ks[0], (BATCH, SEQ, D_MODEL)),
        "x_sample": nrm(ks[1], (DEC_BATCH, DEC_SEQ, D_MODEL)),
        "cache_k": nrm(ks[2], (DEPTH, DEC_BATCH, win_rows, N_KV_HEADS, HEAD_DIM)),
        "cache_v": nrm(ks[3], (DEPTH, DEC_BATCH, win_rows, N_KV_HEADS, HEAD_DIM)),
        "state_pool": nrm(ks[4], (DEPTH, DEC_BATCH, POOL_PAD, POOL_WIDTH)),
        "p_prompt": nrm(ks[5], (DEPTH, BATCH, SEQ, PLE_DIM)),
        "p_sample": nrm(ks[6], (DEPTH, DEC_BATCH, DEC_SEQ, PLE_DIM)),
        "rel_bias_table": nrm(ks[7], (N_BUCKETS, N_Q_HEADS), 0.5),
        "g_mix_pre": 1.0 + nrm(ks[8], (DEPTH, D_MODEL), 0.05),
        "w_in": nrm(ks[9], (DEPTH, D_MODEL, IN_WIDTH), D_MODEL ** -0.5),
        "attn_sinks": nrm(ks[10], (DEPTH, N_Q_HEADS), 0.5),
        "w_pool": nrm(ks[11], (DEPTH, N_POOL_GROUPS, POOL_GROUP_WIDTH, POOL_GROUP_WIDTH), POOL_GROUP_WIDTH ** -0.5),
        "pool_scale": 1.0 + nrm(ks[12], (DEPTH, POOL_WIDTH), 0.05),
        "w_out": nrm(ks[13], (DEPTH, MIX_WIDTH, D_MODEL), MIX_WIDTH ** -0.5),
        "g_mix_post": 1.0 + nrm(ks[14], (DEPTH, D_MODEL), 0.05),
        "g_ffn_pre": 1.0 + nrm(ks[15], (DEPTH, D_MODEL), 0.05),
        "w_ffn_gate": nrm(ks[16], (DEPTH, D_MODEL, D_FF), D_MODEL ** -0.5),
        "w_ffn_up": nrm(ks[17], (DEPTH, D_MODEL, D_FF), D_MODEL ** -0.5),
        "w_ffn_down": nrm(ks[18], (DEPTH, D_FF, D_MODEL), D_FF ** -0.5),
        "g_ffn_post": 1.0 + nrm(ks[19], (DEPTH, D_MODEL), 0.05),
        "w_ple": nrm(ks[20], (DEPTH, PLE_DIM, D_MODEL), PLE_DIM ** -0.5),
        "w_ple_gate": nrm(ks[21], (DEPTH, D_MODEL, D_MODEL), D_MODEL ** -0.5),
    }


def reference(x_prompt, x_sample, cache_k, cache_v, state_pool, p_prompt, p_sample, rel_bias_table,
              g_mix_pre, w_in, attn_sinks, w_pool, pool_scale, w_out, g_mix_post, g_ffn_pre,
              w_ffn_gate, w_ffn_up, w_ffn_down, g_ffn_post, w_ple, w_ple_gate):
    xp, xs = x_prompt, x_sample
    S = xp.shape[1]
    L = xs.shape[1]
    win_prompt = min(WINDOW, S)
    pos_prompt = jnp.arange(S, dtype=jnp.int32)
    pos_sample = PAST_LEN + jnp.arange(L, dtype=jnp.int32)
    kp_list, vp_list, up_list, ks_list, vs_list, us_list = [], [], [], [], [], []
    for i in range(DEPTH):
        q, k, v, u = _split_proj(xp, g_mix_pre[i], w_in[i])
        a = _prompt_attention(q, k, v, rel_bias_table, attn_sinks[i])
        ext = jnp.pad(u, ((0, 0), (POOL_PAD, 0), (0, 0)))
        pl = _multiscale_pool(ext, pos_prompt, w_pool[i], pool_scale[i])
        kp_list.append(k.reshape(k.shape[0], S, N_KV_HEADS, HEAD_DIM)[:, S - win_prompt:])
        vp_list.append(v.reshape(v.shape[0], S, N_KV_HEADS, HEAD_DIM)[:, S - win_prompt:])
        up_list.append(u[:, S - POOL_PAD:])
        xp = _finish_layer(xp, a, pl, p_prompt[i], w_out[i], g_mix_post[i], g_ffn_pre[i],
                           w_ffn_gate[i], w_ffn_up[i], w_ffn_down[i], g_ffn_post[i], w_ple[i], w_ple_gate[i])
        q, k, v, u = _split_proj(xs, g_mix_pre[i], w_in[i])
        a = _sample_attention(q, k, v, cache_k[i], cache_v[i], rel_bias_table, attn_sinks[i])
        ext = jnp.concatenate([state_pool[i].astype(u.dtype), u], axis=1)
        pl = _multiscale_pool(ext, pos_sample, w_pool[i], pool_scale[i])
        ks_list.append(k.reshape(k.shape[0], L, N_KV_HEADS, HEAD_DIM))
        vs_list.append(v.reshape(v.shape[0], L, N_KV_HEADS, HEAD_DIM))
        us_list.append(ext[:, ext.shape[1] - POOL_PAD:])
        xs = _finish_layer(xs, a, pl, p_sample[i], w_out[i], g_mix_post[i], g_ffn_pre[i],
                           w_ffn_gate[i], w_ffn_up[i], w_ffn_down[i], g_ffn_post[i], w_ple[i], w_ple_gate[i])
    new_k_prompt = jnp.stack(kp_list)
    new_v_prompt = jnp.stack(vp_list)
    new_pool_prompt = jnp.stack(up_list)
    new_k_sample = jnp.stack(ks_list)
    new_v_sample = jnp.stack(vs_list)
    new_pool_sample = jnp.stack(us_list)
    return (xp, xs, new_k_prompt, new_v_prompt, new_pool_prompt, new_k_sample, new_v_sample, new_pool_sample)
```

```python
import functools
import math

import jax
import jax.numpy as jnp
import numpy as np
from jax import lax
from jax.experimental import pallas as pl
from jax.experimental.pallas import tpu as pltpu

D_MODEL = 1024
CHUNK = 64
HEAD_DIM = 64
N_Q_HEADS = 8
N_KV_HEADS = 2
GQA_GROUP = N_Q_HEADS // N_KV_HEADS
WINDOW = 128
WIN_CHUNKS = WINDOW // CHUNK
ATTN_WIDTH = N_Q_HEADS * HEAD_DIM
KV_WIDTH = N_KV_HEADS * HEAD_DIM
POOL_WINDOWS = (2, 4, 8, 16)
POOL_WIDTH = D_MODEL // 2
POOL_GROUP_WIDTH = POOL_WIDTH // len(POOL_WINDOWS)
POOL_PAD = max(POOL_WINDOWS) - 1
POOL_HALO = POOL_PAD + 1
MIX_WIDTH = ATTN_WIDTH + POOL_WIDTH
IN_WIDTH = ATTN_WIDTH + 2 * KV_WIDTH + POOL_WIDTH
PLE_DIM = 256
N_BUCKETS = 32
MAX_DISTANCE = 128
RMS_EPS = 1e-6
MASK_VALUE = -1e30
PAST_LEN = 1024

LANES = 128
PAIR_ROWS = GQA_GROUP
VMEM_LIMIT_BYTES = 56 * 1024 * 1024

PROMPT_TILE = 512
FFN_TILE = 512
FF_CHUNK = 256

F32 = jnp.float32
BF16 = jnp.bfloat16


def _rms(x, g):
    ms = jnp.mean(x * x, axis=-1, keepdims=True)
    return x * lax.rsqrt(ms + RMS_EPS) * g


def _dot(a, b):
    return jnp.dot(a, b, preferred_element_type=F32)


def _t5_bucket(rel):
    half = N_BUCKETS // 2
    max_exact = half // 2
    ret = jnp.where(rel > 0, half, 0)
    n = jnp.abs(rel)
    nf = jnp.maximum(n, 1).astype(jnp.float32)
    large = max_exact + (jnp.log(nf / max_exact) / math.log(MAX_DISTANCE / max_exact)
                         * (half - max_exact)).astype(jnp.int32)
    large = jnp.minimum(large, half - 1)
    return ret + jnp.where(n < max_exact, n, large)


def _lane_low(shape):
    return lax.broadcasted_iota(jnp.int32, shape, len(shape) - 1) < HEAD_DIM


def _build_bias(tbl_ref, bidx_ref, bias_s):
    idx = bidx_ref[...]
    lq = idx.shape[0]
    for h in range(N_Q_HEADS):
        acc = jnp.zeros(idx.shape, F32)
        for bk in range(N_BUCKETS):
            acc = jnp.where(idx == bk, tbl_ref[bk, h], acc)
        j, hl = divmod(h, GQA_GROUP)
        bias_s[j, hl * lq:(hl + 1) * lq, :] = acc


def _sink_col(sink_ref, j, lq):
    blk = lax.broadcasted_iota(jnp.int32, (GQA_GROUP * lq, 1), 0) // lq
    col = jnp.zeros((GQA_GROUP * lq, 1), F32)
    for hl in range(GQA_GROUP):
        col = jnp.where(blk == hl, sink_ref[j * GQA_GROUP + hl], col)
    return col


def _dup_halves(x):
    xr = pltpu.roll(x, HEAD_DIM, axis=x.ndim - 1)
    lo = _lane_low(x.shape)
    return jnp.where(lo, x, xr), jnp.where(lo, xr, x)


def _stack_queries(q_s, j, nb, lq):
    blocks = []
    for pp in range(2):
        c0 = (2 * j + pp) * LANES
        qp = q_s[:, c0:c0 + LANES].reshape(nb, lq, LANES)
        lo = _lane_low(qp.shape)
        zero = jnp.zeros_like(qp)
        blocks.append(jnp.where(lo, qp, zero))
        blocks.append(jnp.where(lo, zero, qp))
    return jnp.concatenate(blocks, axis=1)


def _softmax_pv(s, sink, vs):
    m = jnp.maximum(jnp.max(s, axis=-1, keepdims=True), sink)
    p = jnp.exp(s - m)
    den = jnp.sum(p, axis=-1, keepdims=True) + jnp.exp(sink - m)
    o = jnp.einsum('bqk,bkd->bqd', p.astype(BF16), vs, preferred_element_type=F32)
    return o / den


def _unstack_heads(o, lq):
    nb = o.shape[0]
    lo = _lane_low((nb, lq, LANES))
    pairs = []
    for pp in range(2):
        r0 = 2 * pp * lq
        pr = jnp.where(lo, o[:, r0:r0 + lq], o[:, r0 + lq:r0 + 2 * lq])
        pairs.append(pr.reshape(nb * lq, LANES))
    return pairs


def _pool_group(e, g, cnt, halo):
    s = e
    for k in (1, 2, 4, 8)[:g + 1]:
        s = s + pltpu.roll(s, k, axis=0)
    return s[halo:] / cnt - e[halo:]


def _mixer_prompt_kernel(tbl_ref, sink_ref, bidx_ref, x_ref, g1_ref, win_ref, wpool_ref,
                         pscale_ref, wout_ref, g2_ref,
                         x1_ref, nk_ref, nv_ref, npool_ref,
                         bias_s, q_s, k2_s, v2_s, ue_s, mix_s):
    b = pl.program_id(0)
    t = pl.program_id(1)
    nt = pl.num_programs(1)
    tile = x_ref.shape[0]
    nc = tile // CHUNK
    nk = (WIN_CHUNKS + 1) * CHUNK

    @pl.when((b == 0) & (t == 0))
    def _():
        _build_bias(tbl_ref, bidx_ref, bias_s)

    @pl.when(t == 0)
    def _():
        k2_s[:, :WINDOW, :] = jnp.zeros((N_KV_HEADS, WINDOW, LANES), BF16)
        v2_s[:, :WINDOW, :] = jnp.zeros((N_KV_HEADS, WINDOW, LANES), BF16)
        ue_s[:POOL_HALO, :] = jnp.zeros((POOL_HALO, POOL_WIDTH), F32)

    @pl.when(t > 0)
    def _():
        k2_s[:, :WINDOW, :] = k2_s[:, tile:tile + WINDOW, :]
        v2_s[:, :WINDOW, :] = v2_s[:, tile:tile + WINDOW, :]
        ue_s[:POOL_HALO, :] = ue_s[tile:tile + POOL_HALO, :]

    x = x_ref[...]
    h = _rms(x, g1_ref[...]).astype(BF16)

    q_s[...] = (_dot(h, win_ref[:, :ATTN_WIDTH]) * (HEAD_DIM ** -0.5)).astype(BF16)
    zkv = _dot(h, win_ref[:, ATTN_WIDTH:ATTN_WIDTH + 2 * KV_WIDTH])
    k = zkv[:, :KV_WIDTH]
    v = zkv[:, KV_WIDTH:]
    zu = _dot(h, win_ref[:, ATTN_WIDTH + 2 * KV_WIDTH:])
    ue_s[POOL_HALO:, :] = zu

    @pl.when(t == nt - 1)
    def _():
        nk_ref[...] = k[tile - WINDOW:]
        nv_ref[...] = v[tile - WINDOW:]
        npool_ref[...] = zu[tile - POOL_HALO:]

    for src, dst in ((k, k2_s), (v, v2_s)):
        d0, d1 = _dup_halves(src)
        dst[0, WINDOW:, :] = d0.astype(BF16)
        dst[1, WINDOW:, :] = d1.astype(BF16)

    cidx = lax.broadcasted_iota(jnp.int32, (nc, 1, nk), 0)
    sidx = lax.broadcasted_iota(jnp.int32, (nc, 1, nk), 2)
    valid = ((t * nc + cidx) * CHUNK + sidx - WIN_CHUNKS * CHUNK) >= 0

    for j in range(N_KV_HEADS):
        qs = _stack_queries(q_s, j, nc, CHUNK)
        ks = jnp.stack([k2_s[j, c * CHUNK:c * CHUNK + nk, :] for c in range(nc)])
        vs = jnp.stack([v2_s[j, c * CHUNK:c * CHUNK + nk, :] for c in range(nc)])
        s = jnp.einsum('bqd,bkd->bqk', qs, ks, preferred_element_type=F32)
        s = s + bias_s[j][None]
        s = jnp.where(valid, s, MASK_VALUE)
        o = _softmax_pv(s, _sink_col(sink_ref, j, CHUNK)[None], vs)
        for pp, pr in enumerate(_unstack_heads(o, CHUNK)):
            c0 = (2 * j + pp) * LANES
            mix_s[:, c0:c0 + LANES] = pr.astype(BF16)

    pos = t * tile + lax.broadcasted_iota(jnp.int32, (tile, 1), 0)
    for g, w in enumerate(POOL_WINDOWS):
        c0 = g * POOL_GROUP_WIDTH
        cnt = jnp.minimum(pos + 1, w).astype(F32)
        diff = _pool_group(ue_s[:, c0:c0 + POOL_GROUP_WIDTH], g, cnt, POOL_HALO)
        po = _dot(diff.astype(BF16), wpool_ref[g]) * pscale_ref[:, c0:c0 + POOL_GROUP_WIDTH]
        mix_s[:, ATTN_WIDTH + c0:ATTN_WIDTH + c0 + POOL_GROUP_WIDTH] = po.astype(BF16)

    mix = _dot(mix_s[...], wout_ref[...])
    x1_ref[...] = x + _rms(mix, g2_ref[...])


def _mixer_sample_kernel(tbl_ref, sink_ref, bidx_ref, x_ref, ck_ref, cv_ref, st_ref, g1_ref,
                         win_ref, wpool_ref, pscale_ref, wout_ref, g2_ref,
                         x1_ref, nk_ref, nv_ref, nu_ref,
                         bias_s, q_s, ue_s, mix_s, *, nb, lq, past_len):
    rows = nb * lq
    ext = POOL_HALO + lq
    _build_bias(tbl_ref, bidx_ref, bias_s)

    x = x_ref[...]
    h = _rms(x, g1_ref[...]).astype(BF16)
    q_s[...] = (_dot(h, win_ref[:, :ATTN_WIDTH]) * (HEAD_DIM ** -0.5)).astype(BF16)
    zkv = _dot(h, win_ref[:, ATTN_WIDTH:ATTN_WIDTH + 2 * KV_WIDTH])
    k = zkv[:, :KV_WIDTH]
    v = zkv[:, KV_WIDTH:]
    zu = _dot(h, win_ref[:, ATTN_WIDTH + 2 * KV_WIDTH:])
    nk_ref[...] = k
    nv_ref[...] = v
    nu_ref[...] = zu

    n_past = ck_ref.shape[1]
    kc = _dup_halves(ck_ref[...].reshape(nb * n_past, LANES))
    vc = _dup_halves(cv_ref[...].reshape(nb * n_past, LANES))
    kn = _dup_halves(k)
    vn = _dup_halves(v)
    for j in range(N_KV_HEADS):
        qs = _stack_queries(q_s, j, nb, lq)
        ks = jnp.concatenate([kc[j].astype(BF16).reshape(nb, n_past, LANES),
                              kn[j].astype(BF16).reshape(nb, lq, LANES)], axis=1)
        vs = jnp.concatenate([vc[j].astype(BF16).reshape(nb, n_past, LANES),
                              vn[j].astype(BF16).reshape(nb, lq, LANES)], axis=1)
        s = jnp.einsum('bqd,bkd->bqk', qs, ks, preferred_element_type=F32)
        s = s + bias_s[j][None]
        o = _softmax_pv(s, _sink_col(sink_ref, j, lq)[None], vs)
        for pp, pr in enumerate(_unstack_heads(o, lq)):
            c0 = (2 * j + pp) * LANES
            mix_s[:, c0:c0 + LANES] = pr.astype(BF16)

    ue_s[...] = jnp.zeros(ue_s.shape, F32)
    for i in range(nb):
        ue_s[i * ext + 1:i * ext + POOL_HALO, :] = st_ref[i]
        ue_s[i * ext + POOL_HALO:(i + 1) * ext, :] = zu[i * lq:(i + 1) * lq, :]
    pos = past_len + lax.broadcasted_iota(jnp.int32, (nb, lq, 1), 1)
    for g, w in enumerate(POOL_WINDOWS):
        c0 = g * POOL_GROUP_WIDTH
        s = ue_s[:, c0:c0 + POOL_GROUP_WIDTH]
        for sh in (1, 2, 4, 8)[:g + 1]:
            s = s + pltpu.roll(s, sh, axis=0)
        s3 = s.reshape(nb, ext, POOL_GROUP_WIDTH)[:, POOL_HALO:, :]
        u3 = zu[:, c0:c0 + POOL_GROUP_WIDTH].reshape(nb, lq, POOL_GROUP_WIDTH)
        cnt = jnp.minimum(pos + 1, w).astype(F32)
        diff = (s3 / cnt - u3).reshape(rows, POOL_GROUP_WIDTH)
        po = _dot(diff.astype(BF16), wpool_ref[g]) * pscale_ref[:, c0:c0 + POOL_GROUP_WIDTH]
        mix_s[:, ATTN_WIDTH + c0:ATTN_WIDTH + c0 + POOL_GROUP_WIDTH] = po.astype(BF16)

    mix = _dot(mix_s[...], wout_ref[...])
    x1_ref[...] = x + _rms(mix, g2_ref[...])


def _ffn_kernel(x1_ref, pe_ref, g3_ref, wg_ref, wu_ref, wd_ref, g4_ref, wple_ref, wpg_ref,
                out_ref, act_s):
    x1 = x1_ref[...]
    h = _rms(x1, g3_ref[...]).astype(BF16)
    d_ff = wg_ref.shape[1]
    for c0 in range(0, d_ff, FF_CHUNK):
        gate = _dot(h, wg_ref[:, c0:c0 + FF_CHUNK])
        up = _dot(h, wu_ref[:, c0:c0 + FF_CHUNK])
        act_s[:, c0:c0 + FF_CHUNK] = (gate * jax.nn.sigmoid(gate) * up).astype(BF16)
    f = _dot(act_s[...], wd_ref[...])
    x2 = x1 + _rms(f, g4_ref[...])
    gate2 = jax.nn.sigmoid(_dot(x2.astype(BF16), wpg_ref[...]))
    pe = _dot(pe_ref[...].astype(BF16), wple_ref[...])
    out_ref[...] = x2 + gate2 * pe


def _const_spec(shape):
    nd = len(shape)
    return pl.BlockSpec(shape, lambda *_: (0,) * nd, pipeline_mode=pl.Buffered(1))


_SMEM_SPEC = pl.BlockSpec(memory_space=pltpu.SMEM)


def _mixer_prompt(x, tbl, sinks, bidx, g1, win, wpool, pscale, wout, g2):
    bsz, seq, d = x.shape
    tile = PROMPT_TILE
    nk = (WIN_CHUNKS + 1) * CHUNK
    row_spec = pl.BlockSpec((None, tile, d), lambda b, t: (b, t, 0))
    tail = lambda r, w: pl.BlockSpec((None, r, w), lambda b, t: (b, 0, 0))
    return pl.pallas_call(
        _mixer_prompt_kernel,
        grid=(bsz, seq // tile),
        in_specs=[_SMEM_SPEC, _SMEM_SPEC, _const_spec(bidx.shape), row_spec, _const_spec(g1.shape),
                  _const_spec(win.shape), _const_spec(wpool.shape), _const_spec(pscale.shape),
                  _const_spec(wout.shape), _const_spec(g2.shape)],
        out_specs=[row_spec, tail(WINDOW, KV_WIDTH), tail(WINDOW, KV_WIDTH), tail(POOL_HALO, POOL_WIDTH)],
        out_shape=[jax.ShapeDtypeStruct((bsz, seq, d), F32),
                   jax.ShapeDtypeStruct((bsz, WINDOW, KV_WIDTH), F32),
                   jax.ShapeDtypeStruct((bsz, WINDOW, KV_WIDTH), F32),
                   jax.ShapeDtypeStruct((bsz, POOL_HALO, POOL_WIDTH), F32)],
        scratch_shapes=[pltpu.VMEM((N_KV_HEADS, PAIR_ROWS * CHUNK, nk), F32),
                        pltpu.VMEM((tile, ATTN_WIDTH), BF16),
                        pltpu.VMEM((N_KV_HEADS, tile + WINDOW, LANES), BF16),
                        pltpu.VMEM((N_KV_HEADS, tile + WINDOW, LANES), BF16),
                        pltpu.VMEM((tile + POOL_HALO, POOL_WIDTH), F32),
                        pltpu.VMEM((tile, MIX_WIDTH), BF16)],
        compiler_params=pltpu.CompilerParams(dimension_semantics=("arbitrary", "arbitrary"),
                                             vmem_limit_bytes=VMEM_LIMIT_BYTES),
        name="mixer_prompt",
    )(tbl, sinks, bidx, x, g1, win, wpool, pscale, wout, g2)


def _mixer_sample(x, ck, cv, st, tbl, sinks, bidx, g1, win, wpool, pscale, wout, g2):
    nb, lq, d = x.shape
    rows = nb * lq
    n_past = ck.shape[1]
    full = lambda shape: pl.BlockSpec(shape, lambda i: (0,) * len(shape))
    kern = functools.partial(_mixer_sample_kernel, nb=nb, lq=lq, past_len=PAST_LEN)
    args = (tbl, sinks, bidx, x.reshape(rows, d), ck, cv, st, g1, win, wpool, pscale, wout, g2)
    return pl.pallas_call(
        kern,
        grid=(1,),
        in_specs=[_SMEM_SPEC, _SMEM_SPEC] + [full(a.shape) for a in args[2:]],
        out_specs=[full((rows, d)), full((rows, KV_WIDTH)), full((rows, KV_WIDTH)), full((rows, POOL_WIDTH))],
        out_shape=[jax.ShapeDtypeStruct((rows, d), F32),
                   jax.ShapeDtypeStruct((rows, KV_WIDTH), F32),
                   jax.ShapeDtypeStruct((rows, KV_WIDTH), F32),
                   jax.ShapeDtypeStruct((rows, POOL_WIDTH), F32)],
        scratch_shapes=[pltpu.VMEM((N_KV_HEADS, PAIR_ROWS * lq, n_past + lq), F32),
                        pltpu.VMEM((rows, ATTN_WIDTH), BF16),
                        pltpu.VMEM((nb * (POOL_HALO + lq), POOL_WIDTH), F32),
                        pltpu.VMEM((rows, MIX_WIDTH), BF16)],
        compiler_params=pltpu.CompilerParams(dimension_semantics=("arbitrary",),
                                             vmem_limit_bytes=VMEM_LIMIT_BYTES),
        name="mixer_sample",
    )(*args)


def _ffn(x1, pe, g3, wg, wu, wd, g4, wple, wpg):
    rows, d = x1.shape
    tile = min(FFN_TILE, rows)
    row_spec = lambda w: pl.BlockSpec((tile, w), lambda i: (i, 0))
    return pl.pallas_call(
        _ffn_kernel,
        grid=(rows // tile,),
        in_specs=[row_spec(d), row_spec(pe.shape[1]), _const_spec(g3.shape), _const_spec(wg.shape),
                  _const_spec(wu.shape), _const_spec(wd.shape), _const_spec(g4.shape),
                  _const_spec(wple.shape), _const_spec(wpg.shape)],
        out_specs=row_spec(d),
        out_shape=jax.ShapeDtypeStruct((rows, d), F32),
        scratch_shapes=[pltpu.VMEM((tile, wg.shape[1]), BF16)],
        compiler_params=pltpu.CompilerParams(dimension_semantics=("arbitrary",),
                                             vmem_limit_bytes=VMEM_LIMIT_BYTES),
        name="ffn",
    )(x1, pe, g3, wg, wu, wd, g4, wple, wpg)


def kernel(x_prompt, x_sample, cache_k, cache_v, state_pool, p_prompt, p_sample, rel_bias_table,
           g_mix_pre, w_in, attn_sinks, w_pool, pool_scale, w_out, g_mix_post, g_ffn_pre,
           w_ffn_gate, w_ffn_up, w_ffn_down, g_ffn_post, w_ple, w_ple_gate):
    depth = w_in.shape[0]
    assert depth == 1, "single-layer trunk"
    bsz, seq, d = x_prompt.shape
    nb, lq, _ = x_sample.shape
    n_past = cache_k.shape[2]
    assert seq % PROMPT_TILE == 0 and seq >= WINDOW and PAST_LEN - n_past >= 0

    i = 0
    row = lambda g: g[i].reshape(1, -1)
    win = w_in[i].astype(BF16)
    wpool = w_pool[i].astype(BF16)
    wout = w_out[i].astype(BF16)
    wg = w_ffn_gate[i].astype(BF16)
    wu = w_ffn_up[i].astype(BF16)
    wd = w_ffn_down[i].astype(BF16)
    wple = w_ple[i].astype(BF16)
    wpg = w_ple_gate[i].astype(BF16)
    g1, g2, g3, g4 = row(g_mix_pre), row(g_mix_post), row(g_ffn_pre), row(g_ffn_post)
    pscale = row(pool_scale)
    sinks = attn_sinks[i]

    nk = (WIN_CHUNKS + 1) * CHUNK
    qp = jnp.arange(CHUNK, dtype=jnp.int32)
    kp = jnp.arange(nk, dtype=jnp.int32) - WIN_CHUNKS * CHUNK
    bidx_prompt = _t5_bucket(kp[None, :] - qp[:, None])
    qs = PAST_LEN + jnp.arange(lq, dtype=jnp.int32)
    ks = PAST_LEN - n_past + jnp.arange(n_past + lq, dtype=jnp.int32)
    bidx_sample = _t5_bucket(ks[None, :] - qs[:, None])

    x1p, nkp, nvp, nup = _mixer_prompt(x_prompt, rel_bias_table, sinks, bidx_prompt, g1, win,
                                       wpool, pscale, wout, g2)
    yp = _ffn(x1p.reshape(bsz * seq, d), p_prompt[i].reshape(bsz * seq, PLE_DIM), g3, wg, wu, wd,
              g4, wple, wpg).reshape(bsz, seq, d)

    ck = cache_k[i].reshape(nb, n_past, KV_WIDTH)
    cv = cache_v[i].reshape(nb, n_past, KV_WIDTH)
    x1s, nks, nvs, nus = _mixer_sample(x_sample, ck, cv, state_pool[i], rel_bias_table, sinks,
                                       bidx_sample, g1, win, wpool, pscale, wout, g2)
    ys = _ffn(x1s, p_sample[i].reshape(nb * lq, PLE_DIM), g3, wg, wu, wd, g4, wple,
              wpg).reshape(nb, lq, d)

    new_k_prompt = nkp.reshape(1, bsz, WINDOW, N_KV_HEADS, HEAD_DIM)
    new_v_prompt = nvp.reshape(1, bsz, WINDOW, N_KV_HEADS, HEAD_DIM)
    new_pool_prompt = nup[:, POOL_HALO - POOL_PAD:, :][None]
    new_k_sample = nks.reshape(1, nb, lq, N_KV_HEADS, HEAD_DIM)
    new_v_sample = nvs.reshape(1, nb, lq, N_KV_HEADS, HEAD_DIM)
    new_pool_sample = nus.reshape(nb, lq, POOL_WIDTH)[:, lq - POOL_PAD:, :][None]
    return (yp, ys, new_k_prompt, new_v_prompt, new_pool_prompt, new_k_sample, new_v_sample,
            new_pool_sample)
```

```python
import functools
import math

import jax
import jax.numpy as jnp
import numpy as np
from jax import lax
from jax.experimental import pallas as pl
from jax.experimental.pallas import tpu as pltpu

D_MODEL = 1024
CHUNK = 64
HEAD_DIM = 64
N_Q_HEADS = 8
N_KV_HEADS = 2
GQA_GROUP = N_Q_HEADS // N_KV_HEADS
WINDOW = 128
WIN_CHUNKS = WINDOW // CHUNK
ATTN_WIDTH = N_Q_HEADS * HEAD_DIM
KV_WIDTH = N_KV_HEADS * HEAD_DIM
POOL_WINDOWS = (2, 4, 8, 16)
POOL_WIDTH = D_MODEL // 2
POOL_GROUP_WIDTH = POOL_WIDTH // len(POOL_WINDOWS)
POOL_PAD = max(POOL_WINDOWS) - 1
POOL_HALO = POOL_PAD + 1
MIX_WIDTH = ATTN_WIDTH + POOL_WIDTH
IN_WIDTH = ATTN_WIDTH + 2 * KV_WIDTH + POOL_WIDTH
PLE_DIM = 256
N_BUCKETS = 32
MAX_DISTANCE = 128
RMS_EPS = 1e-6
MASK_VALUE = -1e30
PAST_LEN = 1024

LANES = 128
PAIR_ROWS = GQA_GROUP
VMEM_LIMIT_BYTES = 56 * 1024 * 1024

PROMPT_TILE = 512
FFN_TILE = 512
FF_CHUNK = 256

F32 = jnp.float32
BF16 = jnp.bfloat16


def _rms(x, g):
    ms = jnp.mean(x * x, axis=-1, keepdims=True)
    return x * lax.rsqrt(ms + RMS_EPS) * g


def _dot(a, b):
    return jnp.dot(a, b, preferred_element_type=F32)


def _t5_bucket(rel):
    half = N_BUCKETS // 2
    max_exact = half // 2
    ret = jnp.where(rel > 0, half, 0)
    n = jnp.abs(rel)
    nf = jnp.maximum(n, 1).astype(jnp.float32)
    large = max_exact + (jnp.log(nf / max_exact) / math.log(MAX_DISTANCE / max_exact)
                         * (half - max_exact)).astype(jnp.int32)
    large = jnp.minimum(large, half - 1)
    return ret + jnp.where(n < max_exact, n, large)


def _lane_low(shape):
    return lax.broadcasted_iota(jnp.int32, shape, len(shape) - 1) < HEAD_DIM


def _build_bias(tbl_ref, bidx_ref, bias_s):
    idx = bidx_ref[...]
    lq = idx.shape[0]
    for h in range(N_Q_HEADS):
        acc = jnp.zeros(idx.shape, F32)
        for bk in range(N_BUCKETS):
            acc = jnp.where(idx == bk, tbl_ref[bk, h], acc)
        j, hl = divmod(h, GQA_GROUP)
        bias_s[j, hl * lq:(hl + 1) * lq, :] = acc


def _sink_col(sink_ref, j, lq):
    blk = lax.broadcasted_iota(jnp.int32, (GQA_GROUP * lq, 1), 0) // lq
    col = jnp.zeros((GQA_GROUP * lq, 1), F32)
    for hl in range(GQA_GROUP):
        col = jnp.where(blk == hl, sink_ref[j * GQA_GROUP + hl], col)
    return col


def _dup_halves(x):
    xr = pltpu.roll(x, HEAD_DIM, axis=x.ndim - 1)
    lo = _lane_low(x.shape)
    return jnp.where(lo, x, xr), jnp.where(lo, xr, x)


def _stack_queries(q_s, j, nb, lq):
    blocks = []
    for pp in range(2):
        c0 = (2 * j + pp) * LANES
        qp = q_s[:, c0:c0 + LANES].reshape(nb, lq, LANES)
        lo = _lane_low(qp.shape)
        zero = jnp.zeros_like(qp)
        blocks.append(jnp.where(lo, qp, zero))
        blocks.append(jnp.where(lo, zero, qp))
    return jnp.concatenate(blocks, axis=1)


def _softmax(s, sink):
    m = jnp.maximum(jnp.max(s, axis=-1, keepdims=True), sink)
    p = jnp.exp(s - m)
    den = jnp.sum(p, axis=-1, keepdims=True) + jnp.exp(sink - m)
    return p.astype(BF16), den


def _softmax_pv(s, sink, vs):
    p, den = _softmax(s, sink)
    return jnp.einsum('bqk,bkd->bqd', p, vs, preferred_element_type=F32) / den


def _unstack_heads(o, lq):
    nb = o.shape[0]
    lo = _lane_low((nb, lq, LANES))
    pairs = []
    for pp in range(2):
        r0 = 2 * pp * lq
        pr = jnp.where(lo, o[:, r0:r0 + lq], o[:, r0 + lq:r0 + 2 * lq])
        pairs.append(pr.reshape(nb * lq, LANES))
    return pairs


def _pool_group(e, g, cnt, halo):
    s = e
    for k in (1, 2, 4, 8)[:g + 1]:
        s = s + pltpu.roll(s, k, axis=0)
    return s[halo:] / cnt - e[halo:]


def _ffn_body(x1, pe_ref, g3_ref, wg_ref, wu_ref, wd_ref, g4_ref, wple_ref, wpg_ref, out_ref, act_s):
    h = _rms(x1, g3_ref[...]).astype(BF16)
    d_ff = wg_ref.shape[1]
    for c0 in range(0, d_ff, FF_CHUNK):
        gate = _dot(h, wg_ref[:, c0:c0 + FF_CHUNK])
        up = _dot(h, wu_ref[:, c0:c0 + FF_CHUNK])
        act_s[:, c0:c0 + FF_CHUNK] = (gate * jax.nn.sigmoid(gate) * up).astype(BF16)
    f = _dot(act_s[...], wd_ref[...])
    x2 = x1 + _rms(f, g4_ref[...])
    gate2 = jax.nn.sigmoid(_dot(x2.astype(BF16), wpg_ref[...]))
    pe = _dot(pe_ref[...].astype(BF16), wple_ref[...])
    out_ref[...] = x2 + gate2 * pe


def _layer_prompt_kernel(tbl_ref, sink_ref, bidx_ref, x_ref, pe_ref, g1_ref, win_ref, wpool_ref,
                         pscale_ref, wout_ref, g2_ref, g3_ref, wg_ref, wu_ref, wd_ref, g4_ref,
                         wple_ref, wpg_ref,
                         out_ref, nk_ref, nv_ref, npool_ref,
                         bias_s, q_s, k2_s, v2_s, ue_s, mix_s, x1_s, act_s, *, nt):
    s = pl.program_id(0)
    last = pl.num_programs(0) - 2
    t = lax.rem(jnp.minimum(s, last), nt)
    slot_w = lax.rem(s, 2)
    tile = x_ref.shape[0]
    nc = tile // CHUNK
    nk = (WIN_CHUNKS + 1) * CHUNK

    @pl.when(s == 0)
    def _():
        _build_bias(tbl_ref, bidx_ref, bias_s)
        k2_s[...] = jnp.zeros(k2_s.shape, BF16)
        v2_s[...] = jnp.zeros(v2_s.shape, BF16)
        ue_s[...] = jnp.zeros(ue_s.shape, F32)
        x1_s[...] = jnp.zeros(x1_s.shape, F32)

    x1 = x1_s[1 - slot_w]
    d_ff = wg_ref.shape[1]
    ffn = {}
    mixer = {}

    def ffn_norm():
        ffn['h'] = _rms(x1, g3_ref[...]).astype(BF16)

    def ffn_chunk(c):
        c0 = c * FF_CHUNK
        gate = _dot(ffn['h'], wg_ref[:, c0:c0 + FF_CHUNK])
        up = _dot(ffn['h'], wu_ref[:, c0:c0 + FF_CHUNK])
        act_s[:, c0:c0 + FF_CHUNK] = (gate * jax.nn.sigmoid(gate) * up).astype(BF16)

    def ffn_down():
        f = _dot(act_s[...], wd_ref[...])
        ffn['x2'] = x1 + _rms(f, g4_ref[...])

    def ffn_ple():
        x2 = ffn['x2']
        gate2 = jax.nn.sigmoid(_dot(x2.astype(BF16), wpg_ref[...]))
        pe = _dot(pe_ref[...].astype(BF16), wple_ref[...])
        out_ref[...] = x2 + gate2 * pe

    def mixer_in():
        keep = t > 0
        for ref, n_keep in ((k2_s, WINDOW), (v2_s, WINDOW)):
            prev = ref[:, tile:tile + n_keep, :]
            ref[:, :n_keep, :] = jnp.where(keep, prev, jnp.zeros_like(prev))
        prev = ue_s[tile:tile + POOL_HALO, :]
        ue_s[:POOL_HALO, :] = jnp.where(keep, prev, jnp.zeros_like(prev))

        h = _rms(x_ref[...], g1_ref[...]).astype(BF16)
        q_s[...] = (_dot(h, win_ref[:, :ATTN_WIDTH]) * (HEAD_DIM ** -0.5)).astype(BF16)
        zkv = _dot(h, win_ref[:, ATTN_WIDTH:ATTN_WIDTH + 2 * KV_WIDTH])
        k = zkv[:, :KV_WIDTH]
        v = zkv[:, KV_WIDTH:]
        zu = _dot(h, win_ref[:, ATTN_WIDTH + 2 * KV_WIDTH:])
        ue_s[POOL_HALO:, :] = zu
        nk_ref[...] = k[tile - WINDOW:]
        nv_ref[...] = v[tile - WINDOW:]
        npool_ref[...] = zu[tile - POOL_HALO:]
        for src, dst in ((k, k2_s), (v, v2_s)):
            d0, d1 = _dup_halves(src)
            dst[0, WINDOW:, :] = d0.astype(BF16)
            dst[1, WINDOW:, :] = d1.astype(BF16)

    def attn_scores(j):
        cidx = lax.broadcasted_iota(jnp.int32, (nc, 1, nk), 0)
        sidx = lax.broadcasted_iota(jnp.int32, (nc, 1, nk), 2)
        valid = ((t * nc + cidx) * CHUNK + sidx - WIN_CHUNKS * CHUNK) >= 0
        qs = _stack_queries(q_s, j, nc, CHUNK)
        ks = jnp.stack([k2_s[j, c * CHUNK:c * CHUNK + nk, :] for c in range(nc)])
        sc = jnp.einsum('bqd,bkd->bqk', qs, ks, preferred_element_type=F32)
        sc = sc + bias_s[j][None]
        sc = jnp.where(valid, sc, MASK_VALUE)
        mixer['p', j] = _softmax(sc, _sink_col(sink_ref, j, CHUNK)[None])

    def attn_values(j):
        p, den = mixer.pop(('p', j))
        vs = jnp.stack([v2_s[j, c * CHUNK:c * CHUNK + nk, :] for c in range(nc)])
        o = jnp.einsum('bqk,bkd->bqd', p, vs, preferred_element_type=F32) / den
        for pp, pr in enumerate(_unstack_heads(o, CHUNK)):
            c0 = (2 * j + pp) * LANES
            mix_s[:, c0:c0 + LANES] = pr.astype(BF16)

    def pool():
        pos = t * tile + lax.broadcasted_iota(jnp.int32, (tile, 1), 0)
        for g, w in enumerate(POOL_WINDOWS):
            c0 = g * POOL_GROUP_WIDTH
            cnt = jnp.minimum(pos + 1, w).astype(F32)
            diff = _pool_group(ue_s[:, c0:c0 + POOL_GROUP_WIDTH], g, cnt, POOL_HALO)
            po = _dot(diff.astype(BF16), wpool_ref[g]) * pscale_ref[:, c0:c0 + POOL_GROUP_WIDTH]
            mix_s[:, ATTN_WIDTH + c0:ATTN_WIDTH + c0 + POOL_GROUP_WIDTH] = po.astype(BF16)

    def mixer_out():
        mix = _dot(mix_s[...], wout_ref[...])
        x1_s[slot_w] = x_ref[...] + _rms(mix, g2_ref[...])

    n_ff = d_ff // FF_CHUNK
    order = [ffn_norm, mixer_in, (ffn_chunk, 0), (ffn_chunk, 1), (attn_scores, 0),
             (ffn_chunk, 2), (ffn_chunk, 3), (attn_values, 0), (attn_scores, 1),
             (ffn_chunk, 4), (ffn_chunk, 5), (attn_values, 1), pool]
    order += [(ffn_chunk, c) for c in range(6, n_ff)]
    order += [mixer_out, ffn_down, ffn_ple]
    for step in order:
        if isinstance(step, tuple):
            step[0](step[1])
        else:
            step()


def _mixer_sample_kernel(tbl_ref, sink_ref, bidx_ref, x_ref, ck_ref, cv_ref, st_ref, g1_ref,
                         win_ref, wpool_ref, pscale_ref, wout_ref, g2_ref,
                         x1_ref, nk_ref, nv_ref, nu_ref,
                         bias_s, q_s, ue_s, mix_s, *, nb, lq, past_len):
    rows = nb * lq
    ext = POOL_HALO + lq
    _build_bias(tbl_ref, bidx_ref, bias_s)

    x = x_ref[...]
    h = _rms(x, g1_ref[...]).astype(BF16)
    q_s[...] = (_dot(h, win_ref[:, :ATTN_WIDTH]) * (HEAD_DIM ** -0.5)).astype(BF16)
    zkv = _dot(h, win_ref[:, ATTN_WIDTH:ATTN_WIDTH + 2 * KV_WIDTH])
    k = zkv[:, :KV_WIDTH]
    v = zkv[:, KV_WIDTH:]
    zu = _dot(h, win_ref[:, ATTN_WIDTH + 2 * KV_WIDTH:])
    nk_ref[...] = k
    nv_ref[...] = v
    nu_ref[...] = zu

    n_past = ck_ref.shape[1]
    kc = _dup_halves(ck_ref[...].reshape(nb * n_past, LANES))
    vc = _dup_halves(cv_ref[...].reshape(nb * n_past, LANES))
    kn = _dup_halves(k)
    vn = _dup_halves(v)
    for j in range(N_KV_HEADS):
        qs = _stack_queries(q_s, j, nb, lq)
        ks = jnp.concatenate([kc[j].astype(BF16).reshape(nb, n_past, LANES),
                              kn[j].astype(BF16).reshape(nb, lq, LANES)], axis=1)
        vs = jnp.concatenate([vc[j].astype(BF16).reshape(nb, n_past, LANES),
                              vn[j].astype(BF16).reshape(nb, lq, LANES)], axis=1)
        s = jnp.einsum('bqd,bkd->bqk', qs, ks, preferred_element_type=F32)
        s = s + bias_s[j][None]
        o = _softmax_pv(s, _sink_col(sink_ref, j, lq)[None], vs)
        for pp, pr in enumerate(_unstack_heads(o, lq)):
            c0 = (2 * j + pp) * LANES
            mix_s[:, c0:c0 + LANES] = pr.astype(BF16)

    ue_s[...] = jnp.zeros(ue_s.shape, F32)
    for i in range(nb):
        ue_s[i * ext + 1:i * ext + POOL_HALO, :] = st_ref[i]
        ue_s[i * ext + POOL_HALO:(i + 1) * ext, :] = zu[i * lq:(i + 1) * lq, :]
    pos = past_len + lax.broadcasted_iota(jnp.int32, (nb, lq, 1), 1)
    for g, w in enumerate(POOL_WINDOWS):
        c0 = g * POOL_GROUP_WIDTH
        s = ue_s[:, c0:c0 + POOL_GROUP_WIDTH]
        for sh in (1, 2, 4, 8)[:g + 1]:
            s = s + pltpu.roll(s, sh, axis=0)
        s3 = s.reshape(nb, ext, POOL_GROUP_WIDTH)[:, POOL_HALO:, :]
        u3 = zu[:, c0:c0 + POOL_GROUP_WIDTH].reshape(nb, lq, POOL_GROUP_WIDTH)
        cnt = jnp.minimum(pos + 1, w).astype(F32)
        diff = (s3 / cnt - u3).reshape(rows, POOL_GROUP_WIDTH)
        po = _dot(diff.astype(BF16), wpool_ref[g]) * pscale_ref[:, c0:c0 + POOL_GROUP_WIDTH]
        mix_s[:, ATTN_WIDTH + c0:ATTN_WIDTH + c0 + POOL_GROUP_WIDTH] = po.astype(BF16)

    mix = _dot(mix_s[...], wout_ref[...])
    x1_ref[...] = x + _rms(mix, g2_ref[...])


def _ffn_kernel(x1_ref, pe_ref, g3_ref, wg_ref, wu_ref, wd_ref, g4_ref, wple_ref, wpg_ref,
                out_ref, act_s):
    _ffn_body(x1_ref[...], pe_ref, g3_ref, wg_ref, wu_ref, wd_ref, g4_ref, wple_ref, wpg_ref,
              out_ref, act_s)


def _const_spec(shape):
    nd = len(shape)
    return pl.BlockSpec(shape, lambda *_: (0,) * nd, pipeline_mode=pl.Buffered(1))


_SMEM_SPEC = pl.BlockSpec(memory_space=pltpu.SMEM)


def _layer_prompt(x, pe, tbl, sinks, bidx, g1, win, wpool, pscale, wout, g2, g3, wg, wu, wd, g4, wple, wpg):
    bsz, seq, d = x.shape
    tile = PROMPT_TILE
    nt = seq // tile
    n_tiles = bsz * nt
    nk = (WIN_CHUNKS + 1) * CHUNK
    mixer_tile = lambda s: jnp.minimum(s, n_tiles - 1)
    ffn_tile = lambda s: jnp.maximum(s - 1, 0)
    x_spec = pl.BlockSpec((None, tile, d), lambda s: (mixer_tile(s) // nt, mixer_tile(s) % nt, 0))
    tail = lambda r, w: pl.BlockSpec((None, r, w), lambda s: (mixer_tile(s) // nt, 0, 0))
    ffn_rows = lambda w: pl.BlockSpec((tile, w), lambda s: (ffn_tile(s), 0))
    consts = (g1, win, wpool, pscale, wout, g2, g3, wg, wu, wd, g4, wple, wpg)
    return pl.pallas_call(
        functools.partial(_layer_prompt_kernel, nt=nt),
        grid=(n_tiles + 1,),
        in_specs=[_SMEM_SPEC, _SMEM_SPEC, _const_spec(bidx.shape), x_spec, ffn_rows(pe.shape[1])]
                 + [_const_spec(c.shape) for c in consts],
        out_specs=[ffn_rows(d), tail(WINDOW, KV_WIDTH), tail(WINDOW, KV_WIDTH), tail(POOL_HALO, POOL_WIDTH)],
        out_shape=[jax.ShapeDtypeStruct((bsz * seq, d), F32),
                   jax.ShapeDtypeStruct((bsz, WINDOW, KV_WIDTH), F32),
                   jax.ShapeDtypeStruct((bsz, WINDOW, KV_WIDTH), F32),
                   jax.ShapeDtypeStruct((bsz, POOL_HALO, POOL_WIDTH), F32)],
        scratch_shapes=[pltpu.VMEM((N_KV_HEADS, PAIR_ROWS * CHUNK, nk), F32),
                        pltpu.VMEM((tile, ATTN_WIDTH), BF16),
                        pltpu.VMEM((N_KV_HEADS, tile + WINDOW, LANES), BF16),
                        pltpu.VMEM((N_KV_HEADS, tile + WINDOW, LANES), BF16),
                        pltpu.VMEM((tile + POOL_HALO, POOL_WIDTH), F32),
                        pltpu.VMEM((tile, MIX_WIDTH), BF16),
                        pltpu.VMEM((2, tile, d), F32),
                        pltpu.VMEM((tile, wg.shape[1]), BF16)],
        compiler_params=pltpu.CompilerParams(dimension_semantics=("arbitrary",),
                                             vmem_limit_bytes=VMEM_LIMIT_BYTES),
        name="layer_prompt",
    )(tbl, sinks, bidx, x, pe, *consts)


def _mixer_sample(x, ck, cv, st, tbl, sinks, bidx, g1, win, wpool, pscale, wout, g2):
    nb, lq, d = x.shape
    rows = nb * lq
    n_past = ck.shape[1]
    full = lambda shape: pl.BlockSpec(shape, lambda i: (0,) * len(shape))
    kern = functools.partial(_mixer_sample_kernel, nb=nb, lq=lq, past_len=PAST_LEN)
    args = (tbl, sinks, bidx, x.reshape(rows, d), ck, cv, st, g1, win, wpool, pscale, wout, g2)
    return pl.pallas_call(
        kern,
        grid=(1,),
        in_specs=[_SMEM_SPEC, _SMEM_SPEC] + [full(a.shape) for a in args[2:]],
        out_specs=[full((rows, d)), full((rows, KV_WIDTH)), full((rows, KV_WIDTH)), full((rows, POOL_WIDTH))],
        out_shape=[jax.ShapeDtypeStruct((rows, d), F32),
                   jax.ShapeDtypeStruct((rows, KV_WIDTH), F32),
                   jax.ShapeDtypeStruct((rows, KV_WIDTH), F32),
                   jax.ShapeDtypeStruct((rows, POOL_WIDTH), F32)],
        scratch_shapes=[pltpu.VMEM((N_KV_HEADS, PAIR_ROWS * lq, n_past + lq), F32),
                        pltpu.VMEM((rows, ATTN_WIDTH), BF16),
                        pltpu.VMEM((nb * (POOL_HALO + lq), POOL_WIDTH), F32),
                        pltpu.VMEM((rows, MIX_WIDTH), BF16)],
        compiler_params=pltpu.CompilerParams(dimension_semantics=("arbitrary",),
                                             vmem_limit_bytes=VMEM_LIMIT_BYTES),
        name="mixer_sample",
    )(*args)


def _ffn(x1, pe, g3, wg, wu, wd, g4, wple, wpg):
    rows, d = x1.shape
    tile = min(FFN_TILE, rows)
    row_spec = lambda w: pl.BlockSpec((tile, w), lambda i: (i, 0))
    return pl.pallas_call(
        _ffn_kernel,
        grid=(rows // tile,),
        in_specs=[row_spec(d), row_spec(pe.shape[1]), _const_spec(g3.shape), _const_spec(wg.shape),
                  _const_spec(wu.shape), _const_spec(wd.shape), _const_spec(g4.shape),
                  _const_spec(wple.shape), _const_spec(wpg.shape)],
        out_specs=row_spec(d),
        out_shape=jax.ShapeDtypeStruct((rows, d), F32),
        scratch_shapes=[pltpu.VMEM((tile, wg.shape[1]), BF16)],
        compiler_params=pltpu.CompilerParams(dimension_semantics=("arbitrary",),
                                             vmem_limit_bytes=VMEM_LIMIT_BYTES),
        name="ffn",
    )(x1, pe, g3, wg, wu, wd, g4, wple, wpg)


def kernel(x_prompt, x_sample, cache_k, cache_v, state_pool, p_prompt, p_sample, rel_bias_table,
           g_mix_pre, w_in, attn_sinks, w_pool, pool_scale, w_out, g_mix_post, g_ffn_pre,
           w_ffn_gate, w_ffn_up, w_ffn_down, g_ffn_post, w_ple, w_ple_gate):
    depth = w_in.shape[0]
    assert depth == 1, "single-layer trunk"
    bsz, seq, d = x_prompt.shape
    nb, lq, _ = x_sample.shape
    n_past = cache_k.shape[2]
    assert seq % PROMPT_TILE == 0 and seq >= WINDOW and PAST_LEN - n_past >= 0

    i = 0
    row = lambda g: g[i].reshape(1, -1)
    win = w_in[i].astype(BF16)
    wpool = w_pool[i].astype(BF16)
    wout = w_out[i].astype(BF16)
    wg = w_ffn_gate[i].astype(BF16)
    wu = w_ffn_up[i].astype(BF16)
    wd = w_ffn_down[i].astype(BF16)
    wple = w_ple[i].astype(BF16)
    wpg = w_ple_gate[i].astype(BF16)
    g1, g2, g3, g4 = row(g_mix_pre), row(g_mix_post), row(g_ffn_pre), row(g_ffn_post)
    pscale = row(pool_scale)
    sinks = attn_sinks[i]

    nk = (WIN_CHUNKS + 1) * CHUNK
    qp = jnp.arange(CHUNK, dtype=jnp.int32)
    kp = jnp.arange(nk, dtype=jnp.int32) - WIN_CHUNKS * CHUNK
    bidx_prompt = _t5_bucket(kp[None, :] - qp[:, None])
    qs = PAST_LEN + jnp.arange(lq, dtype=jnp.int32)
    ks = PAST_LEN - n_past + jnp.arange(n_past + lq, dtype=jnp.int32)
    bidx_sample = _t5_bucket(ks[None, :] - qs[:, None])

    yp, nkp, nvp, nup = _layer_prompt(x_prompt, p_prompt[i].reshape(bsz * seq, PLE_DIM), rel_bias_table,
                                      sinks, bidx_prompt, g1, win, wpool, pscale, wout, g2, g3, wg, wu,
                                      wd, g4, wple, wpg)
    yp = yp.reshape(bsz, seq, d)

    ck = cache_k[i].reshape(nb, n_past, KV_WIDTH)
    cv = cache_v[i].reshape(nb, n_past, KV_WIDTH)
    x1s, nks, nvs, nus = _mixer_sample(x_sample, ck, cv, state_pool[i], rel_bias_table, sinks,
                                       bidx_sample, g1, win, wpool, pscale, wout, g2)
    ys = _ffn(x1s, p_sample[i].reshape(nb * lq, PLE_DIM), g3, wg, wu, wd, g4, wple,
              wpg).reshape(nb, lq, d)

    new_k_prompt = nkp.reshape(1, bsz, WINDOW, N_KV_HEADS, HEAD_DIM)
    new_v_prompt = nvp.reshape(1, bsz, WINDOW, N_KV_HEADS, HEAD_DIM)
    new_pool_prompt = nup[:, POOL_HALO - POOL_PAD:, :][None]
    new_k_sample = nks.reshape(1, nb, lq, N_KV_HEADS, HEAD_DIM)
    new_v_sample = nvs.reshape(1, nb, lq, N_KV_HEADS, HEAD_DIM)
    new_pool_sample = nus.reshape(nb, lq, POOL_WIDTH)[:, lq - POOL_PAD:, :][None]
    return (yp, ys, new_k_prompt, new_v_prompt, new_pool_prompt, new_k_sample, new_v_sample,
            new_pool_sample)
```

```python
import functools
import math

import jax
import jax.numpy as jnp
import numpy as np
from jax import lax
from jax.experimental import pallas as pl
from jax.experimental.pallas import tpu as pltpu

D_MODEL = 1024
CHUNK = 64
HEAD_DIM = 64
N_Q_HEADS = 8
N_KV_HEADS = 2
GQA_GROUP = N_Q_HEADS // N_KV_HEADS
WINDOW = 128
WIN_CHUNKS = WINDOW // CHUNK
ATTN_WIDTH = N_Q_HEADS * HEAD_DIM
KV_WIDTH = N_KV_HEADS * HEAD_DIM
POOL_WINDOWS = (2, 4, 8, 16)
POOL_WIDTH = D_MODEL // 2
POOL_GROUP_WIDTH = POOL_WIDTH // len(POOL_WINDOWS)
POOL_PAD = max(POOL_WINDOWS) - 1
POOL_HALO = POOL_PAD + 1
MIX_WIDTH = ATTN_WIDTH + POOL_WIDTH
IN_WIDTH = ATTN_WIDTH + 2 * KV_WIDTH + POOL_WIDTH
PLE_DIM = 256
N_BUCKETS = 32
MAX_DISTANCE = 128
RMS_EPS = 1e-6
MASK_VALUE = -1e30
PAST_LEN = 1024

LANES = 128
PAIR_ROWS = GQA_GROUP
VMEM_LIMIT_BYTES = 56 * 1024 * 1024

PROMPT_TILE = 512
STAGE_ROWS = 128
FF_CHUNK = 256

F32 = jnp.float32
BF16 = jnp.bfloat16


def _rms(x, g):
    ms = jnp.mean(x * x, axis=-1, keepdims=True)
    return x * lax.rsqrt(ms + RMS_EPS) * g


def _dot(a, b):
    return jnp.dot(a, b, preferred_element_type=F32)


def _t5_bucket(rel):
    half = N_BUCKETS // 2
    max_exact = half // 2
    ret = jnp.where(rel > 0, half, 0)
    n = jnp.abs(rel)
    nf = jnp.maximum(n, 1).astype(jnp.float32)
    large = max_exact + (jnp.log(nf / max_exact) / math.log(MAX_DISTANCE / max_exact)
                         * (half - max_exact)).astype(jnp.int32)
    large = jnp.minimum(large, half - 1)
    return ret + jnp.where(n < max_exact, n, large)


def _lane_low(shape):
    return lax.broadcasted_iota(jnp.int32, shape, len(shape) - 1) < HEAD_DIM


def _build_bias(tbl_ref, bidx_ref, bias_s):
    idx = bidx_ref[...]
    lq = idx.shape[0]
    for h in range(N_Q_HEADS):
        acc = jnp.zeros(idx.shape, F32)
        for bk in range(N_BUCKETS):
            acc = jnp.where(idx == bk, tbl_ref[bk, h], acc)
        j, hl = divmod(h, GQA_GROUP)
        bias_s[j, hl * lq:(hl + 1) * lq, :] = acc


def _sink_col(sink_ref, j, lq):
    blk = lax.broadcasted_iota(jnp.int32, (GQA_GROUP * lq, 1), 0) // lq
    col = jnp.zeros((GQA_GROUP * lq, 1), F32)
    for hl in range(GQA_GROUP):
        col = jnp.where(blk == hl, sink_ref[j * GQA_GROUP + hl], col)
    return col


def _dup_halves(x):
    xr = pltpu.roll(x, HEAD_DIM, axis=x.ndim - 1)
    lo = _lane_low(x.shape)
    return jnp.where(lo, x, xr), jnp.where(lo, xr, x)


def _stack_queries(q_s, j, nb, lq):
    blocks = []
    for pp in range(2):
        c0 = (2 * j + pp) * LANES
        qp = q_s[:, c0:c0 + LANES].reshape(nb, lq, LANES)
        lo = _lane_low(qp.shape)
        zero = jnp.zeros_like(qp)
        blocks.append(jnp.where(lo, qp, zero))
        blocks.append(jnp.where(lo, zero, qp))
    return jnp.concatenate(blocks, axis=1)


def _softmax(s, sink):
    m = jnp.maximum(jnp.max(s, axis=-1, keepdims=True), sink)
    p = jnp.exp(s - m)
    den = jnp.sum(p, axis=-1, keepdims=True) + jnp.exp(sink - m)
    return p.astype(BF16), den


def _softmax_pv(s, sink, vs):
    p, den = _softmax(s, sink)
    return jnp.einsum('bqk,bkd->bqd', p, vs, preferred_element_type=F32) / den


def _unstack_heads(o, lq):
    nb = o.shape[0]
    lo = _lane_low((nb, lq, LANES))
    pairs = []
    for pp in range(2):
        r0 = 2 * pp * lq
        pr = jnp.where(lo, o[:, r0:r0 + lq], o[:, r0 + lq:r0 + 2 * lq])
        pairs.append(pr.reshape(nb * lq, LANES))
    return pairs


def _pool_group(e, g, cnt, halo):
    s = e
    for k in (1, 2, 4, 8)[:g + 1]:
        s = s + pltpu.roll(s, k, axis=0)
    return s[halo:] / cnt - e[halo:]


def _load_weights_bf16(pairs, stage, sem):
    jobs = []
    for src, dst in pairs:
        rows, cols = src.shape
        for r0 in range(0, rows, STAGE_ROWS):
            jobs.append((src, dst, r0, min(STAGE_ROWS, rows - r0), cols))

    def copy(i):
        src, _, r0, nr, cols = jobs[i]
        return pltpu.make_async_copy(src.at[pl.ds(r0, nr), :],
                                     stage.at[i % 2, pl.ds(0, nr), pl.ds(0, cols)], sem.at[i % 2])

    copy(0).start()
    for i, (_, dst, r0, nr, cols) in enumerate(jobs):
        if i + 1 < len(jobs):
            copy(i + 1).start()
        copy(i).wait()
        dst[r0:r0 + nr, :] = stage[i % 2, :nr, :cols].astype(BF16)


def _layer_prompt_kernel(tbl_ref, sink_ref, bidx_ref, x_ref, pe_ref, pe_smp_ref, g1_ref, pscale_ref,
                         g2_ref, g3_ref, g4_ref, x1_smp_hbm,
                         win_hbm, wpool_hbm, wout_hbm, wg_hbm, wu_hbm, wd_hbm, wple_hbm, wpg_hbm,
                         out_ref, nk_ref, nv_ref, npool_ref, out_smp_hbm,
                         bias_s, q_s, k2_s, v2_s, ue_s, mix_s, x1_s, act_s,
                         win_ref, wpool_ref, wout_ref, wg_ref, wu_ref, wd_ref, wple_ref, wpg_ref,
                         stage_s, sem, *, nt):
    s = pl.program_id(0)
    last = pl.num_programs(0) - 2
    t = lax.rem(jnp.minimum(s, last), nt)
    slot_w = lax.rem(s, 2)
    tile = x_ref.shape[0]
    nc = tile // CHUNK
    nk = (WIN_CHUNKS + 1) * CHUNK

    @pl.when(s == 0)
    def _():
        pairs = [(win_hbm, win_ref), (wout_hbm, wout_ref), (wg_hbm, wg_ref), (wu_hbm, wu_ref),
                 (wd_hbm, wd_ref), (wple_hbm, wple_ref), (wpg_hbm, wpg_ref)]
        pairs += [(wpool_hbm.at[g], wpool_ref.at[g]) for g in range(len(POOL_WINDOWS))]
        _load_weights_bf16(pairs, stage_s, sem)
        pltpu.sync_copy(x1_smp_hbm, x1_s.at[1])
        _build_bias(tbl_ref, bidx_ref, bias_s)
        k2_s[...] = jnp.zeros(k2_s.shape, BF16)
        v2_s[...] = jnp.zeros(v2_s.shape, BF16)
        ue_s[...] = jnp.zeros(ue_s.shape, F32)

    x1 = x1_s[1 - slot_w]
    pe_in = jnp.where(s == 0, pe_smp_ref[...], pe_ref[...])
    d_ff = wg_ref.shape[1]
    ffn = {}
    mixer = {}

    def ffn_norm():
        ffn['h'] = _rms(x1, g3_ref[...]).astype(BF16)

    def ffn_chunk(c):
        c0 = c * FF_CHUNK
        gate = _dot(ffn['h'], wg_ref[:, c0:c0 + FF_CHUNK])
        up = _dot(ffn['h'], wu_ref[:, c0:c0 + FF_CHUNK])
        act_s[:, c0:c0 + FF_CHUNK] = (gate * jax.nn.sigmoid(gate) * up).astype(BF16)

    def ffn_down():
        f = _dot(act_s[...], wd_ref[...])
        ffn['x2'] = x1 + _rms(f, g4_ref[...])

    def ffn_ple():
        x2 = ffn['x2']
        gate2 = jax.nn.sigmoid(_dot(x2.astype(BF16), wpg_ref[...]))
        pe = _dot(pe_in.astype(BF16), wple_ref[...])
        out_ref[...] = x2 + gate2 * pe

    def mixer_in():
        keep = t > 0
        for ref, n_keep in ((k2_s, WINDOW), (v2_s, WINDOW)):
            prev = ref[:, tile:tile + n_keep, :]
            ref[:, :n_keep, :] = jnp.where(keep, prev, jnp.zeros_like(prev))
        prev = ue_s[tile:tile + POOL_HALO, :]
        ue_s[:POOL_HALO, :] = jnp.where(keep, prev, jnp.zeros_like(prev))

        h = _rms(x_ref[...], g1_ref[...]).astype(BF16)
        q_s[...] = (_dot(h, win_ref[:, :ATTN_WIDTH]) * (HEAD_DIM ** -0.5)).astype(BF16)
        zkv = _dot(h, win_ref[:, ATTN_WIDTH:ATTN_WIDTH + 2 * KV_WIDTH])
        k = zkv[:, :KV_WIDTH]
        v = zkv[:, KV_WIDTH:]
        zu = _dot(h, win_ref[:, ATTN_WIDTH + 2 * KV_WIDTH:])
        ue_s[POOL_HALO:, :] = zu
        nk_ref[...] = k[tile - WINDOW:]
        nv_ref[...] = v[tile - WINDOW:]
        npool_ref[...] = zu[tile - POOL_HALO:]
        for src, dst in ((k, k2_s), (v, v2_s)):
            d0, d1 = _dup_halves(src)
            dst[0, WINDOW:, :] = d0.astype(BF16)
            dst[1, WINDOW:, :] = d1.astype(BF16)

    def attn_scores(j):
        cidx = lax.broadcasted_iota(jnp.int32, (nc, 1, nk), 0)
        sidx = lax.broadcasted_iota(jnp.int32, (nc, 1, nk), 2)
        valid = ((t * nc + cidx) * CHUNK + sidx - WIN_CHUNKS * CHUNK) >= 0
        qs = _stack_queries(q_s, j, nc, CHUNK)
        ks = jnp.stack([k2_s[j, c * CHUNK:c * CHUNK + nk, :] for c in range(nc)])
        sc = jnp.einsum('bqd,bkd->bqk', qs, ks, preferred_element_type=F32)
        sc = sc + bias_s[j][None]
        sc = jnp.where(valid, sc, MASK_VALUE)
        mixer['p', j] = _softmax(sc, _sink_col(sink_ref, j, CHUNK)[None])

    def attn_values(j):
        p, den = mixer.pop(('p', j))
        vs = jnp.stack([v2_s[j, c * CHUNK:c * CHUNK + nk, :] for c in range(nc)])
        o = jnp.einsum('bqk,bkd->bqd', p, vs, preferred_element_type=F32) / den
        for pp, pr in enumerate(_unstack_heads(o, CHUNK)):
            c0 = (2 * j + pp) * LANES
            mix_s[:, c0:c0 + LANES] = pr.astype(BF16)

    def pool():
        pos = t * tile + lax.broadcasted_iota(jnp.int32, (tile, 1), 0)
        for g, w in enumerate(POOL_WINDOWS):
            c0 = g * POOL_GROUP_WIDTH
            cnt = jnp.minimum(pos + 1, w).astype(F32)
            diff = _pool_group(ue_s[:, c0:c0 + POOL_GROUP_WIDTH], g, cnt, POOL_HALO)
            po = _dot(diff.astype(BF16), wpool_ref[g]) * pscale_ref[:, c0:c0 + POOL_GROUP_WIDTH]
            mix_s[:, ATTN_WIDTH + c0:ATTN_WIDTH + c0 + POOL_GROUP_WIDTH] = po.astype(BF16)

    def mixer_out():
        mix = _dot(mix_s[...], wout_ref[...])
        x1_s[slot_w] = x_ref[...] + _rms(mix, g2_ref[...])

    n_ff = d_ff // FF_CHUNK
    order = [ffn_norm, mixer_in, (ffn_chunk, 0), (ffn_chunk, 1), (attn_scores, 0),
             (ffn_chunk, 2), (ffn_chunk, 3), (ffn_chunk, 4), (attn_values, 0), (attn_scores, 1),
             (ffn_chunk, 5), (ffn_chunk, 6), (ffn_chunk, 7), (attn_values, 1), pool]
    order += [(ffn_chunk, c) for c in range(8, n_ff)]
    order += [ffn_down, mixer_out, ffn_ple]
    for step in order:
        if isinstance(step, tuple):
            step[0](step[1])
        else:
            step()

    @pl.when(s == 0)
    def _():
        pltpu.sync_copy(out_ref, out_smp_hbm)


def _mixer_sample_kernel(tbl_ref, sink_ref, bidx_ref, x_ref, ck_ref, cv_ref, st_ref, g1_ref,
                         win_ref, wpool_ref, pscale_ref, wout_ref, g2_ref,
                         x1_ref, nk_ref, nv_ref, nu_ref,
                         bias_s, q_s, ue_s, mix_s, *, nb, lq, past_len):
    rows = nb * lq
    ext = POOL_HALO + lq
    _build_bias(tbl_ref, bidx_ref, bias_s)

    x = x_ref[...]
    h = _rms(x, g1_ref[...]).astype(BF16)
    win = win_ref[...].astype(BF16)
    q_s[...] = (_dot(h, win[:, :ATTN_WIDTH]) * (HEAD_DIM ** -0.5)).astype(BF16)
    zkv = _dot(h, win[:, ATTN_WIDTH:ATTN_WIDTH + 2 * KV_WIDTH])
    k = zkv[:, :KV_WIDTH]
    v = zkv[:, KV_WIDTH:]
    zu = _dot(h, win[:, ATTN_WIDTH + 2 * KV_WIDTH:])
    nk_ref[...] = k
    nv_ref[...] = v
    nu_ref[...] = zu

    n_past = ck_ref.shape[1]
    kc = _dup_halves(ck_ref[...].reshape(nb * n_past, LANES))
    vc = _dup_halves(cv_ref[...].reshape(nb * n_past, LANES))
    kn = _dup_halves(k)
    vn = _dup_halves(v)
    for j in range(N_KV_HEADS):
        qs = _stack_queries(q_s, j, nb, lq)
        ks = jnp.concatenate([kc[j].astype(BF16).reshape(nb, n_past, LANES),
                              kn[j].astype(BF16).reshape(nb, lq, LANES)], axis=1)
        vs = jnp.concatenate([vc[j].astype(BF16).reshape(nb, n_past, LANES),
                              vn[j].astype(BF16).reshape(nb, lq, LANES)], axis=1)
        s = jnp.einsum('bqd,bkd->bqk', qs, ks, preferred_element_type=F32)
        s = s + bias_s[j][None]
        o = _softmax_pv(s, _sink_col(sink_ref, j, lq)[None], vs)
        for pp, pr in enumerate(_unstack_heads(o, lq)):
            c0 = (2 * j + pp) * LANES
            mix_s[:, c0:c0 + LANES] = pr.astype(BF16)

    ue_s[...] = jnp.zeros(ue_s.shape, F32)
    for i in range(nb):
        ue_s[i * ext + 1:i * ext + POOL_HALO, :] = st_ref[i]
        ue_s[i * ext + POOL_HALO:(i + 1) * ext, :] = zu[i * lq:(i + 1) * lq, :]
    pos = past_len + lax.broadcasted_iota(jnp.int32, (nb, lq, 1), 1)
    for g, w in enumerate(POOL_WINDOWS):
        c0 = g * POOL_GROUP_WIDTH
        s = ue_s[:, c0:c0 + POOL_GROUP_WIDTH]
        for sh in (1, 2, 4, 8)[:g + 1]:
            s = s + pltpu.roll(s, sh, axis=0)
        s3 = s.reshape(nb, ext, POOL_GROUP_WIDTH)[:, POOL_HALO:, :]
        u3 = zu[:, c0:c0 + POOL_GROUP_WIDTH].reshape(nb, lq, POOL_GROUP_WIDTH)
        cnt = jnp.minimum(pos + 1, w).astype(F32)
        diff = (s3 / cnt - u3).reshape(rows, POOL_GROUP_WIDTH)
        po = _dot(diff.astype(BF16), wpool_ref[g].astype(BF16)) * pscale_ref[:, c0:c0 + POOL_GROUP_WIDTH]
        mix_s[:, ATTN_WIDTH + c0:ATTN_WIDTH + c0 + POOL_GROUP_WIDTH] = po.astype(BF16)

    mix = _dot(mix_s[...], wout_ref[...].astype(BF16))
    x1_ref[...] = x + _rms(mix, g2_ref[...])


def _const_spec(shape):
    nd = len(shape)
    return pl.BlockSpec(shape, lambda *_: (0,) * nd, pipeline_mode=pl.Buffered(1))


_SMEM_SPEC = pl.BlockSpec(memory_space=pltpu.SMEM)


def _layer_prompt(x, pe, pe_smp, x1_smp, tbl, sinks, bidx, g1, pscale, g2, g3, g4, weights):
    bsz, seq, d = x.shape
    tile = PROMPT_TILE
    nt = seq // tile
    n_tiles = bsz * nt
    nk = (WIN_CHUNKS + 1) * CHUNK
    assert x1_smp.shape == (tile, d) and pe_smp.shape == (tile, pe.shape[1])
    mixer_tile = lambda s: jnp.minimum(s, n_tiles - 1)
    ffn_tile = lambda s: jnp.maximum(s - 1, 0)
    x_spec = pl.BlockSpec((None, tile, d), lambda s: (mixer_tile(s) // nt, mixer_tile(s) % nt, 0))
    tail = lambda r, w: pl.BlockSpec((None, r, w), lambda s: (mixer_tile(s) // nt, 0, 0))
    ffn_rows = lambda w: pl.BlockSpec((tile, w), lambda s: (ffn_tile(s), 0))
    consts = (pe_smp, g1, pscale, g2, g3, g4)
    hbm_spec = pl.BlockSpec(memory_space=pl.ANY)
    stage_cols = max(w.shape[-1] for w in weights)
    return pl.pallas_call(
        functools.partial(_layer_prompt_kernel, nt=nt),
        grid=(n_tiles + 1,),
        in_specs=[_SMEM_SPEC, _SMEM_SPEC, _const_spec(bidx.shape), x_spec, ffn_rows(pe.shape[1])]
                 + [_const_spec(c.shape) for c in consts] + [hbm_spec] * (1 + len(weights)),
        out_specs=[ffn_rows(d), tail(WINDOW, KV_WIDTH), tail(WINDOW, KV_WIDTH),
                   tail(POOL_HALO, POOL_WIDTH), hbm_spec],
        out_shape=[jax.ShapeDtypeStruct((bsz * seq, d), F32),
                   jax.ShapeDtypeStruct((bsz, WINDOW, KV_WIDTH), F32),
                   jax.ShapeDtypeStruct((bsz, WINDOW, KV_WIDTH), F32),
                   jax.ShapeDtypeStruct((bsz, POOL_HALO, POOL_WIDTH), F32),
                   jax.ShapeDtypeStruct((tile, d), F32)],
        scratch_shapes=[pltpu.VMEM((N_KV_HEADS, PAIR_ROWS * CHUNK, nk), F32),
                        pltpu.VMEM((tile, ATTN_WIDTH), BF16),
                        pltpu.VMEM((N_KV_HEADS, tile + WINDOW, LANES), BF16),
                        pltpu.VMEM((N_KV_HEADS, tile + WINDOW, LANES), BF16),
                        pltpu.VMEM((tile + POOL_HALO, POOL_WIDTH), F32),
                        pltpu.VMEM((tile, MIX_WIDTH), BF16),
                        pltpu.VMEM((2, tile, d), F32),
                        pltpu.VMEM((tile, weights[3].shape[1]), BF16)]
                       + [pltpu.VMEM(w.shape, BF16) for w in weights]
                       + [pltpu.VMEM((2, STAGE_ROWS, stage_cols), F32), pltpu.SemaphoreType.DMA((2,))],
        compiler_params=pltpu.CompilerParams(dimension_semantics=("arbitrary",),
                                             vmem_limit_bytes=VMEM_LIMIT_BYTES),
        name="layer_prompt",
    )(tbl, sinks, bidx, x, pe, *consts, x1_smp, *weights)


def _mixer_sample(x, ck, cv, st, tbl, sinks, bidx, g1, win, wpool, pscale, wout, g2):
    nb, lq, d = x.shape
    rows = nb * lq
    n_past = ck.shape[1]
    full = lambda shape: pl.BlockSpec(shape, lambda i: (0,) * len(shape))
    kern = functools.partial(_mixer_sample_kernel, nb=nb, lq=lq, past_len=PAST_LEN)
    args = (tbl, sinks, bidx, x.reshape(rows, d), ck, cv, st, g1, win, wpool, pscale, wout, g2)
    return pl.pallas_call(
        kern,
        grid=(1,),
        in_specs=[_SMEM_SPEC, _SMEM_SPEC] + [full(a.shape) for a in args[2:]],
        out_specs=[full((rows, d)), full((rows, KV_WIDTH)), full((rows, KV_WIDTH)), full((rows, POOL_WIDTH))],
        out_shape=[jax.ShapeDtypeStruct((rows, d), F32),
                   jax.ShapeDtypeStruct((rows, KV_WIDTH), F32),
                   jax.ShapeDtypeStruct((rows, KV_WIDTH), F32),
                   jax.ShapeDtypeStruct((rows, POOL_WIDTH), F32)],
        scratch_shapes=[pltpu.VMEM((N_KV_HEADS, PAIR_ROWS * lq, n_past + lq), F32),
                        pltpu.VMEM((rows, ATTN_WIDTH), BF16),
                        pltpu.VMEM((nb * (POOL_HALO + lq), POOL_WIDTH), F32),
                        pltpu.VMEM((rows, MIX_WIDTH), BF16)],
        compiler_params=pltpu.CompilerParams(dimension_semantics=("arbitrary",),
                                             vmem_limit_bytes=VMEM_LIMIT_BYTES),
        name="mixer_sample",
    )(*args)


def kernel(x_prompt, x_sample, cache_k, cache_v, state_pool, p_prompt, p_sample, rel_bias_table,
           g_mix_pre, w_in, attn_sinks, w_pool, pool_scale, w_out, g_mix_post, g_ffn_pre,
           w_ffn_gate, w_ffn_up, w_ffn_down, g_ffn_post, w_ple, w_ple_gate):
    depth = w_in.shape[0]
    assert depth == 1, "single-layer trunk"
    bsz, seq, d = x_prompt.shape
    nb, lq, _ = x_sample.shape
    n_past = cache_k.shape[2]
    assert seq % PROMPT_TILE == 0 and seq >= WINDOW and PAST_LEN - n_past >= 0
    assert nb * lq == PROMPT_TILE and lq >= POOL_PAD

    i = 0
    row = lambda g: g[i].reshape(1, -1)
    weights = (w_in[i], w_pool[i], w_out[i], w_ffn_gate[i], w_ffn_up[i], w_ffn_down[i], w_ple[i],
               w_ple_gate[i])
    g1, g2, g3, g4 = row(g_mix_pre), row(g_mix_post), row(g_ffn_pre), row(g_ffn_post)
    pscale = row(pool_scale)
    sinks = attn_sinks[i]

    nk = (WIN_CHUNKS + 1) * CHUNK
    qp = jnp.arange(CHUNK, dtype=jnp.int32)
    kp = jnp.arange(nk, dtype=jnp.int32) - WIN_CHUNKS * CHUNK
    bidx_prompt = _t5_bucket(kp[None, :] - qp[:, None])
    qs = PAST_LEN + jnp.arange(lq, dtype=jnp.int32)
    ks = PAST_LEN - n_past + jnp.arange(n_past + lq, dtype=jnp.int32)
    bidx_sample = _t5_bucket(ks[None, :] - qs[:, None])

    ck = cache_k[i].reshape(nb, n_past, KV_WIDTH)
    cv = cache_v[i].reshape(nb, n_past, KV_WIDTH)
    x1s, nks, nvs, nus = _mixer_sample(x_sample, ck, cv, state_pool[i], rel_bias_table, sinks,
                                       bidx_sample, g1, weights[0], weights[1], pscale, weights[2], g2)

    yp, nkp, nvp, nup, ys = _layer_prompt(x_prompt, p_prompt[i].reshape(bsz * seq, PLE_DIM),
                                          p_sample[i].reshape(nb * lq, PLE_DIM), x1s, rel_bias_table,
                                          sinks, bidx_prompt, g1, pscale, g2, g3, g4, weights)
    yp = yp.reshape(bsz, seq, d)
    ys = ys.reshape(nb, lq, d)

    new_k_prompt = nkp.reshape(1, bsz, WINDOW, N_KV_HEADS, HEAD_DIM)
    new_v_prompt = nvp.reshape(1, bsz, WINDOW, N_KV_HEADS, HEAD_DIM)
    new_pool_prompt = nup[:, POOL_HALO - POOL_PAD:, :][None]
    new_k_sample = nks.reshape(1, nb, lq, N_KV_HEADS, HEAD_DIM)
    new_v_sample = nvs.reshape(1, nb, lq, N_KV_HEADS, HEAD_DIM)
    new_pool_sample = nus.reshape(nb, lq, POOL_WIDTH)[:, lq - POOL_PAD:, :][None]
    return (yp, ys, new_k_prompt, new_v_prompt, new_pool_prompt, new_k_sample, new_v_sample,
            new_pool_sample)
```

```python
import functools
import math

import jax
import jax.numpy as jnp
from jax import lax
from jax.experimental import pallas as pl
from jax.experimental.pallas import tpu as pltpu

D_MODEL = 1024
CHUNK = 64
HEAD_DIM = 64
N_Q_HEADS = 8
N_KV_HEADS = 2
GQA_GROUP = N_Q_HEADS // N_KV_HEADS
WINDOW = 128
WIN_CHUNKS = WINDOW // CHUNK
ATTN_WIDTH = N_Q_HEADS * HEAD_DIM
KV_WIDTH = N_KV_HEADS * HEAD_DIM
POOL_WINDOWS = (2, 4, 8, 16)
POOL_WIDTH = D_MODEL // 2
POOL_GROUP_WIDTH = POOL_WIDTH // len(POOL_WINDOWS)
POOL_PAD = max(POOL_WINDOWS) - 1
POOL_HALO = POOL_PAD + 1
MIX_WIDTH = ATTN_WIDTH + POOL_WIDTH
IN_WIDTH = ATTN_WIDTH + 2 * KV_WIDTH + POOL_WIDTH
PLE_DIM = 256
N_BUCKETS = 32
MAX_DISTANCE = 128
RMS_EPS = 1e-6
MASK_VALUE = -1e30
PAST_LEN = 1024

LANES = 128
VMEM_LIMIT_BYTES = 58 * 1024 * 1024

PROMPT_TILE = 512
STAGE_SHAPE = (8, 128, 1024)
FF_CHUNK = 256

F32 = jnp.float32
BF16 = jnp.bfloat16


def _rms(x, g):
    ms = jnp.mean(x * x, axis=-1, keepdims=True)
    return x * lax.rsqrt(ms + RMS_EPS) * g


def _dot(a, b):
    return jnp.dot(a, b, preferred_element_type=F32)


def _t5_bucket(rel):
    half = N_BUCKETS // 2
    max_exact = half // 2
    ret = jnp.where(rel > 0, half, 0)
    n = jnp.abs(rel)
    nf = jnp.maximum(n, 1).astype(jnp.float32)
    large = max_exact + (jnp.log(nf / max_exact) / math.log(MAX_DISTANCE / max_exact)
                         * (half - max_exact)).astype(jnp.int32)
    large = jnp.minimum(large, half - 1)
    return ret + jnp.where(n < max_exact, n, large)


def _lane_low(shape):
    return lax.broadcasted_iota(jnp.int32, shape, len(shape) - 1) < HEAD_DIM


def _rel_bias(q_pos, k_pos, table):
    b = table[_t5_bucket(k_pos[None, :] - q_pos[:, None])].astype(F32)
    b = jnp.transpose(b, (2, 0, 1))
    return b.reshape(N_KV_HEADS, GQA_GROUP * q_pos.shape[0], k_pos.shape[0])


def _sink_col(sink_ref, j, lq):
    blk = lax.broadcasted_iota(jnp.int32, (GQA_GROUP * lq, 1), 0) // lq
    col = jnp.zeros((GQA_GROUP * lq, 1), F32)
    for hl in range(GQA_GROUP):
        col = jnp.where(blk == hl, sink_ref[j * GQA_GROUP + hl], col)
    return col


def _dup_halves(x):
    xr = pltpu.roll(x, HEAD_DIM, axis=x.ndim - 1)
    lo = _lane_low(x.shape)
    return jnp.where(lo, x, xr), jnp.where(lo, xr, x)


def _stack_queries(q_s, j, nb, lq):
    blocks = []
    for pp in range(2):
        c0 = (2 * j + pp) * LANES
        qp = q_s[:, c0:c0 + LANES].reshape(nb, lq, LANES)
        lo = _lane_low(qp.shape)
        zero = jnp.zeros_like(qp)
        blocks.append(jnp.where(lo, qp, zero))
        blocks.append(jnp.where(lo, zero, qp))
    return jnp.concatenate(blocks, axis=1)


def _softmax(s, sink):
    m = jnp.maximum(jnp.max(s, axis=-1, keepdims=True), sink)
    p = jnp.exp(s - m)
    den = jnp.sum(p, axis=-1, keepdims=True) + jnp.exp(sink - m)
    return p.astype(BF16), den


def _softmax_pv(s, sink, vs):
    p, den = _softmax(s, sink)
    return jnp.einsum('bqk,bkd->bqd', p, vs, preferred_element_type=F32) / den


def _unstack_heads(o, lq):
    nb = o.shape[0]
    lo = _lane_low((nb, lq, LANES))
    pairs = []
    for pp in range(2):
        r0 = 2 * pp * lq
        pr = jnp.where(lo, o[:, r0:r0 + lq], o[:, r0 + lq:r0 + 2 * lq])
        pairs.append(pr.reshape(nb * lq, LANES))
    return pairs


def _pool_group(e, g, cnt, halo):
    s = e
    for k in (1, 2, 4, 8)[:g + 1]:
        s = s + pltpu.roll(s, k, axis=0)
    return s[halo:] / cnt - e[halo:]


def _load_weights_bf16(pairs, stage, sem):
    n_slots, srows, scols = stage.shape
    jobs = []
    for src, dst in pairs:
        rows, cols = src.shape
        for r0 in range(0, rows, srows):
            for c0 in range(0, cols, scols):
                jobs.append((src, dst, r0, min(srows, rows - r0), c0, min(scols, cols - c0)))

    def copy(i):
        src, _, r0, nr, c0, ncol = jobs[i]
        slot = i % n_slots
        return pltpu.make_async_copy(src.at[pl.ds(r0, nr), pl.ds(c0, ncol)],
                                     stage.at[slot, pl.ds(0, nr), pl.ds(0, ncol)], sem.at[slot])

    for i in range(min(n_slots - 1, len(jobs))):
        copy(i).start()
    for i, (_, dst, r0, nr, c0, ncol) in enumerate(jobs):
        if i + n_slots - 1 < len(jobs):
            copy(i + n_slots - 1).start()
        copy(i).wait()
        dst[r0:r0 + nr, c0:c0 + ncol] = stage[i % n_slots, :nr, :ncol].astype(BF16)


def _layer_prompt_kernel(sink_ref, bias_ref, x_ref, pe_ref, pe_smp_ref, g1_ref, pscale_ref,
                         g2_ref, g3_ref, g4_ref, x1_smp_hbm,
                         win_hbm, wpool_hbm, wout_hbm, wg_hbm, wu_hbm, wd_hbm, wple_hbm, wpg_hbm,
                         out_ref, nk_ref, nv_ref, npool_ref, out_smp_hbm,
                         q_s, k2_s, v2_s, ue_s, mix_s, x1_s, act_s,
                         win_ref, wpool_ref, wout_ref, wg_ref, wu_ref, wd_ref, wple_ref, wpg_ref,
                         stage_s, sem, *, nt):
    s = pl.program_id(0)
    last = pl.num_programs(0) - 2
    t = lax.rem(jnp.minimum(s, last), nt)
    slot_w = lax.rem(s, 2)
    tile = x_ref.shape[0]
    nc = tile // CHUNK
    nk = (WIN_CHUNKS + 1) * CHUNK

    @pl.when(s == 0)
    def _():
        pairs = [(win_hbm, win_ref), (wout_hbm, wout_ref), (wg_hbm, wg_ref), (wu_hbm, wu_ref),
                 (wd_hbm, wd_ref), (wple_hbm, wple_ref), (wpg_hbm, wpg_ref)]
        pairs += [(wpool_hbm.at[g], wpool_ref.at[g]) for g in range(len(POOL_WINDOWS))]
        _load_weights_bf16(pairs, stage_s, sem)
        pltpu.sync_copy(x1_smp_hbm, x1_s.at[1])
        k2_s[...] = jnp.zeros(k2_s.shape, BF16)
        v2_s[...] = jnp.zeros(v2_s.shape, BF16)
        ue_s[...] = jnp.zeros(ue_s.shape, F32)

    x1 = x1_s[1 - slot_w]
    pe_in = jnp.where(s == 0, pe_smp_ref[...], pe_ref[...])
    d_ff = wg_ref.shape[1]
    ffn = {}
    mixer = {}

    def ffn_norm():
        ffn['h'] = _rms(x1, g3_ref[...]).astype(BF16)

    def ffn_chunk(c):
        c0 = c * FF_CHUNK
        gate = _dot(ffn['h'], wg_ref[:, c0:c0 + FF_CHUNK])
        up = _dot(ffn['h'], wu_ref[:, c0:c0 + FF_CHUNK])
        act_s[:, c0:c0 + FF_CHUNK] = (gate * jax.nn.sigmoid(gate) * up).astype(BF16)

    def ffn_down():
        f = _dot(act_s[...], wd_ref[...])
        ffn['x2'] = x1 + _rms(f, g4_ref[...])

    def ffn_ple():
        x2 = ffn['x2']
        gate2 = jax.nn.sigmoid(_dot(x2.astype(BF16), wpg_ref[...]))
        pe = _dot(pe_in.astype(BF16), wple_ref[...])
        out_ref[...] = x2 + gate2 * pe

    def mixer_in():
        keep = t > 0
        for ref, n_keep in ((k2_s, WINDOW), (v2_s, WINDOW)):
            prev = ref[:, tile:tile + n_keep, :]
            ref[:, :n_keep, :] = jnp.where(keep, prev, jnp.zeros_like(prev))
        prev = ue_s[tile:tile + POOL_HALO, :]
        ue_s[:POOL_HALO, :] = jnp.where(keep, prev, jnp.zeros_like(prev))

        h = _rms(x_ref[...], g1_ref[...]).astype(BF16)
        q_s[...] = (_dot(h, win_ref[:, :ATTN_WIDTH]) * (HEAD_DIM ** -0.5)).astype(BF16)
        zkv = _dot(h, win_ref[:, ATTN_WIDTH:ATTN_WIDTH + 2 * KV_WIDTH])
        k = zkv[:, :KV_WIDTH]
        v = zkv[:, KV_WIDTH:]
        zu = _dot(h, win_ref[:, ATTN_WIDTH + 2 * KV_WIDTH:])
        ue_s[POOL_HALO:, :] = zu
        nk_ref[...] = k[tile - WINDOW:]
        nv_ref[...] = v[tile - WINDOW:]
        npool_ref[...] = zu[tile - POOL_HALO:]
        for src, dst in ((k, k2_s), (v, v2_s)):
            d0, d1 = _dup_halves(src)
            dst[0, WINDOW:, :] = d0.astype(BF16)
            dst[1, WINDOW:, :] = d1.astype(BF16)

    def attn_scores(j):
        cidx = lax.broadcasted_iota(jnp.int32, (nc, 1, nk), 0)
        sidx = lax.broadcasted_iota(jnp.int32, (nc, 1, nk), 2)
        valid = ((t * nc + cidx) * CHUNK + sidx - WIN_CHUNKS * CHUNK) >= 0
        qs = _stack_queries(q_s, j, nc, CHUNK)
        ks = jnp.stack([k2_s[j, c * CHUNK:c * CHUNK + nk, :] for c in range(nc)])
        sc = jnp.einsum('bqd,bkd->bqk', qs, ks, preferred_element_type=F32)
        sc = sc + bias_ref[j][None]
        sc = jnp.where(valid, sc, MASK_VALUE)
        mixer['p', j] = _softmax(sc, _sink_col(sink_ref, j, CHUNK)[None])

    def attn_values(j):
        p, den = mixer.pop(('p', j))
        vs = jnp.stack([v2_s[j, c * CHUNK:c * CHUNK + nk, :] for c in range(nc)])
        o = jnp.einsum('bqk,bkd->bqd', p, vs, preferred_element_type=F32) / den
        for pp, pr in enumerate(_unstack_heads(o, CHUNK)):
            c0 = (2 * j + pp) * LANES
            mix_s[:, c0:c0 + LANES] = pr.astype(BF16)

    def pool():
        pos = t * tile + lax.broadcasted_iota(jnp.int32, (tile, 1), 0)
        for g, w in enumerate(POOL_WINDOWS):
            c0 = g * POOL_GROUP_WIDTH
            cnt = jnp.minimum(pos + 1, w).astype(F32)
            diff = _pool_group(ue_s[:, c0:c0 + POOL_GROUP_WIDTH], g, cnt, POOL_HALO)
            po = _dot(diff.astype(BF16), wpool_ref[g]) * pscale_ref[:, c0:c0 + POOL_GROUP_WIDTH]
            mix_s[:, ATTN_WIDTH + c0:ATTN_WIDTH + c0 + POOL_GROUP_WIDTH] = po.astype(BF16)

    def mixer_out():
        mix = _dot(mix_s[...], wout_ref[...])
        x1_s[slot_w] = x_ref[...] + _rms(mix, g2_ref[...])

    n_ff = d_ff // FF_CHUNK
    order = [ffn_norm, mixer_in, (ffn_chunk, 0), (ffn_chunk, 1), (attn_scores, 0),
             (ffn_chunk, 2), (ffn_chunk, 3), (ffn_chunk, 4), (attn_values, 0), (attn_scores, 1),
             (ffn_chunk, 5), (ffn_chunk, 6), (ffn_chunk, 7), (attn_values, 1), pool]
    order += [(ffn_chunk, c) for c in range(8, n_ff)]
    order += [ffn_down, mixer_out, ffn_ple]
    for step in order:
        if isinstance(step, tuple):
            step[0](step[1])
        else:
            step()

    @pl.when(s == 0)
    def _():
        pltpu.sync_copy(out_ref, out_smp_hbm)


def _mixer_sample_kernel(sink_ref, bias_ref, x_ref, ck_ref, cv_ref, st_ref, g1_ref,
                         win_ref, wpool_ref, pscale_ref, wout_ref, g2_ref,
                         x1_ref, nk_ref, nv_ref, nu_ref,
                         q_s, ue_s, mix_s, *, nb, lq, past_len):
    rows = nb * lq
    ext = POOL_HALO + lq
    x = x_ref[...]
    h = _rms(x, g1_ref[...]).astype(BF16)
    win = win_ref[...].astype(BF16)
    q_s[...] = (_dot(h, win[:, :ATTN_WIDTH]) * (HEAD_DIM ** -0.5)).astype(BF16)
    zkv = _dot(h, win[:, ATTN_WIDTH:ATTN_WIDTH + 2 * KV_WIDTH])
    k = zkv[:, :KV_WIDTH]
    v = zkv[:, KV_WIDTH:]
    zu = _dot(h, win[:, ATTN_WIDTH + 2 * KV_WIDTH:])
    nk_ref[...] = k
    nv_ref[...] = v
    nu_ref[...] = zu

    n_past = ck_ref.shape[1]
    kc = _dup_halves(ck_ref[...].reshape(nb * n_past, LANES))
    vc = _dup_halves(cv_ref[...].reshape(nb * n_past, LANES))
    kn = _dup_halves(k)
    vn = _dup_halves(v)
    for j in range(N_KV_HEADS):
        qs = _stack_queries(q_s, j, nb, lq)
        ks = jnp.concatenate([kc[j].astype(BF16).reshape(nb, n_past, LANES),
                              kn[j].astype(BF16).reshape(nb, lq, LANES)], axis=1)
        vs = jnp.concatenate([vc[j].astype(BF16).reshape(nb, n_past, LANES),
                              vn[j].astype(BF16).reshape(nb, lq, LANES)], axis=1)
        s = jnp.einsum('bqd,bkd->bqk', qs, ks, preferred_element_type=F32)
        s = s + bias_ref[j][None]
        o = _softmax_pv(s, _sink_col(sink_ref, j, lq)[None], vs)
        for pp, pr in enumerate(_unstack_heads(o, lq)):
            c0 = (2 * j + pp) * LANES
            mix_s[:, c0:c0 + LANES] = pr.astype(BF16)

    ue_s[...] = jnp.zeros(ue_s.shape, F32)
    for i in range(nb):
        ue_s[i * ext + 1:i * ext + POOL_HALO, :] = st_ref[i]
        ue_s[i * ext + POOL_HALO:(i + 1) * ext, :] = zu[i * lq:(i + 1) * lq, :]
    pos = past_len + lax.broadcasted_iota(jnp.int32, (nb, lq, 1), 1)
    for g, w in enumerate(POOL_WINDOWS):
        c0 = g * POOL_GROUP_WIDTH
        s = ue_s[:, c0:c0 + POOL_GROUP_WIDTH]
        for sh in (1, 2, 4, 8)[:g + 1]:
            s = s + pltpu.roll(s, sh, axis=0)
        s3 = s.reshape(nb, ext, POOL_GROUP_WIDTH)[:, POOL_HALO:, :]
        u3 = zu[:, c0:c0 + POOL_GROUP_WIDTH].reshape(nb, lq, POOL_GROUP_WIDTH)
        cnt = jnp.minimum(pos + 1, w).astype(F32)
        diff = (s3 / cnt - u3).reshape(rows, POOL_GROUP_WIDTH)
        po = _dot(diff.astype(BF16), wpool_ref[g].astype(BF16)) * pscale_ref[:, c0:c0 + POOL_GROUP_WIDTH]
        mix_s[:, ATTN_WIDTH + c0:ATTN_WIDTH + c0 + POOL_GROUP_WIDTH] = po.astype(BF16)

    mix = _dot(mix_s[...], wout_ref[...].astype(BF16))
    x1_ref[...] = x + _rms(mix, g2_ref[...])


def _const_spec(shape):
    nd = len(shape)
    return pl.BlockSpec(shape, lambda *_: (0,) * nd, pipeline_mode=pl.Buffered(1))


_SMEM_SPEC = pl.BlockSpec(memory_space=pltpu.SMEM)


def _layer_prompt(x, pe, pe_smp, x1_smp, sinks, bias, g1, pscale, g2, g3, g4, weights):
    bsz, seq, d = x.shape
    tile = PROMPT_TILE
    nt = seq // tile
    n_tiles = bsz * nt
    nk = (WIN_CHUNKS + 1) * CHUNK
    assert x1_smp.shape == (tile, d) and pe_smp.shape == (tile, pe.shape[1])
    mixer_tile = lambda s: jnp.minimum(s, n_tiles - 1)
    ffn_tile = lambda s: jnp.maximum(s - 1, 0)
    x_spec = pl.BlockSpec((None, tile, d), lambda s: (mixer_tile(s) // nt, mixer_tile(s) % nt, 0))
    tail = lambda r, w: pl.BlockSpec((None, r, w), lambda s: (mixer_tile(s) // nt, 0, 0))
    ffn_rows = lambda w: pl.BlockSpec((tile, w), lambda s: (ffn_tile(s), 0))
    consts = (pe_smp, g1, pscale, g2, g3, g4)
    hbm_spec = pl.BlockSpec(memory_space=pl.ANY)
    return pl.pallas_call(
        functools.partial(_layer_prompt_kernel, nt=nt),
        grid=(n_tiles + 1,),
        in_specs=[_SMEM_SPEC, _const_spec(bias.shape), x_spec, ffn_rows(pe.shape[1])]
                 + [_const_spec(c.shape) for c in consts] + [hbm_spec] * (1 + len(weights)),
        out_specs=[ffn_rows(d), tail(WINDOW, KV_WIDTH), tail(WINDOW, KV_WIDTH),
                   tail(POOL_HALO, POOL_WIDTH), hbm_spec],
        out_shape=[jax.ShapeDtypeStruct((bsz * seq, d), F32),
                   jax.ShapeDtypeStruct((bsz, WINDOW, KV_WIDTH), F32),
                   jax.ShapeDtypeStruct((bsz, WINDOW, KV_WIDTH), F32),
                   jax.ShapeDtypeStruct((bsz, POOL_HALO, POOL_WIDTH), F32),
                   jax.ShapeDtypeStruct((tile, d), F32)],
        scratch_shapes=[pltpu.VMEM((tile, ATTN_WIDTH), BF16),
                        pltpu.VMEM((N_KV_HEADS, tile + WINDOW, LANES), BF16),
                        pltpu.VMEM((N_KV_HEADS, tile + WINDOW, LANES), BF16),
                        pltpu.VMEM((tile + POOL_HALO, POOL_WIDTH), F32),
                        pltpu.VMEM((tile, MIX_WIDTH), BF16),
                        pltpu.VMEM((2, tile, d), F32),
                        pltpu.VMEM((tile, weights[3].shape[1]), BF16)]
                       + [pltpu.VMEM(w.shape, BF16) for w in weights]
                       + [pltpu.VMEM(STAGE_SHAPE, F32), pltpu.SemaphoreType.DMA((STAGE_SHAPE[0],))],
        compiler_params=pltpu.CompilerParams(dimension_semantics=("arbitrary",),
                                             vmem_limit_bytes=VMEM_LIMIT_BYTES),
        name="layer_prompt",
    )(sinks, bias, x, pe, *consts, x1_smp, *weights)


def _mixer_sample(x, ck, cv, st, sinks, bias, g1, win, wpool, pscale, wout, g2):
    nb, lq, d = x.shape
    rows = nb * lq
    n_past = ck.shape[1]
    full = lambda shape: pl.BlockSpec(shape, lambda i: (0,) * len(shape))
    kern = functools.partial(_mixer_sample_kernel, nb=nb, lq=lq, past_len=PAST_LEN)
    args = (sinks, bias, x.reshape(rows, d), ck, cv, st, g1, win, wpool, pscale, wout, g2)
    return pl.pallas_call(
        kern,
        grid=(1,),
        in_specs=[_SMEM_SPEC] + [full(a.shape) for a in args[1:]],
        out_specs=[full((rows, d)), full((rows, KV_WIDTH)), full((rows, KV_WIDTH)), full((rows, POOL_WIDTH))],
        out_shape=[jax.ShapeDtypeStruct((rows, d), F32),
                   jax.ShapeDtypeStruct((rows, KV_WIDTH), F32),
                   jax.ShapeDtypeStruct((rows, KV_WIDTH), F32),
                   jax.ShapeDtypeStruct((rows, POOL_WIDTH), F32)],
        scratch_shapes=[pltpu.VMEM((rows, ATTN_WIDTH), BF16),
                        pltpu.VMEM((nb * (POOL_HALO + lq), POOL_WIDTH), F32),
                        pltpu.VMEM((rows, MIX_WIDTH), BF16)],
        compiler_params=pltpu.CompilerParams(dimension_semantics=("arbitrary",),
                                             vmem_limit_bytes=VMEM_LIMIT_BYTES),
        name="mixer_sample",
    )(*args)


def kernel(x_prompt, x_sample, cache_k, cache_v, state_pool, p_prompt, p_sample, rel_bias_table,
           g_mix_pre, w_in, attn_sinks, w_pool, pool_scale, w_out, g_mix_post, g_ffn_pre,
           w_ffn_gate, w_ffn_up, w_ffn_down, g_ffn_post, w_ple, w_ple_gate):
    depth = w_in.shape[0]
    assert depth == 1, "single-layer trunk"
    bsz, seq, d = x_prompt.shape
    nb, lq, _ = x_sample.shape
    n_past = cache_k.shape[2]
    assert seq % PROMPT_TILE == 0 and seq >= WINDOW and PAST_LEN - n_past >= 0
    assert nb * lq == PROMPT_TILE and lq >= POOL_PAD

    i = 0
    row = lambda g: g[i].reshape(1, -1)
    weights = (w_in[i], w_pool[i], w_out[i], w_ffn_gate[i], w_ffn_up[i], w_ffn_down[i], w_ple[i],
               w_ple_gate[i])
    g1, g2, g3, g4 = row(g_mix_pre), row(g_mix_post), row(g_ffn_pre), row(g_ffn_post)
    pscale = row(pool_scale)
    sinks = attn_sinks[i]

    nk = (WIN_CHUNKS + 1) * CHUNK
    bias_prompt = _rel_bias(jnp.arange(CHUNK, dtype=jnp.int32),
                            jnp.arange(nk, dtype=jnp.int32) - WIN_CHUNKS * CHUNK, rel_bias_table)
    bias_sample = _rel_bias(PAST_LEN + jnp.arange(lq, dtype=jnp.int32),
                            PAST_LEN - n_past + jnp.arange(n_past + lq, dtype=jnp.int32), rel_bias_table)

    ck = cache_k[i].reshape(nb, n_past, KV_WIDTH)
    cv = cache_v[i].reshape(nb, n_past, KV_WIDTH)
    x1s, nks, nvs, nus = _mixer_sample(x_sample, ck, cv, state_pool[i], sinks, bias_sample, g1,
                                       weights[0], weights[1], pscale, weights[2], g2)

    yp, nkp, nvp, nup, ys = _layer_prompt(x_prompt, p_prompt[i].reshape(bsz * seq, PLE_DIM),
                                          p_sample[i].reshape(nb * lq, PLE_DIM), x1s, sinks,
                                          bias_prompt, g1, pscale, g2, g3, g4, weights)
    yp = yp.reshape(bsz, seq, d)
    ys = ys.reshape(nb, lq, d)

    new_k_prompt = nkp.reshape(1, bsz, WINDOW, N_KV_HEADS, HEAD_DIM)
    new_v_prompt = nvp.reshape(1, bsz, WINDOW, N_KV_HEADS, HEAD_DIM)
    new_pool_prompt = nup[:, POOL_HALO - POOL_PAD:, :][None]
    new_k_sample = nks.reshape(1, nb, lq, N_KV_HEADS, HEAD_DIM)
    new_v_sample = nvs.reshape(1, nb, lq, N_KV_HEADS, HEAD_DIM)
    new_pool_sample = nus.reshape(nb, lq, POOL_WIDTH)[:, lq - POOL_PAD:, :][None]
    return (yp, ys, new_k_prompt, new_v_prompt, new_pool_prompt, new_k_sample, new_v_sample,
            new_pool_sample)
```

```python
import functools
import math

import jax
import jax.numpy as jnp
from jax import lax
from jax.experimental import pallas as pl
from jax.experimental.pallas import tpu as pltpu

D_MODEL = 1024
CHUNK = 64
HEAD_DIM = 64
N_Q_HEADS = 8
N_KV_HEADS = 2
GQA_GROUP = N_Q_HEADS // N_KV_HEADS
WINDOW = 128
WIN_CHUNKS = WINDOW // CHUNK
ATTN_WIDTH = N_Q_HEADS * HEAD_DIM
KV_WIDTH = N_KV_HEADS * HEAD_DIM
POOL_WINDOWS = (2, 4, 8, 16)
POOL_WIDTH = D_MODEL // 2
POOL_GROUP_WIDTH = POOL_WIDTH // len(POOL_WINDOWS)
POOL_PAD = max(POOL_WINDOWS) - 1
POOL_HALO = POOL_PAD + 1
MIX_WIDTH = ATTN_WIDTH + POOL_WIDTH
IN_WIDTH = ATTN_WIDTH + 2 * KV_WIDTH + POOL_WIDTH
PLE_DIM = 256
N_BUCKETS = 32
MAX_DISTANCE = 128
RMS_EPS = 1e-6
MASK_VALUE = -1e30
PAST_LEN = 1024

LANES = 128
VMEM_LIMIT_BYTES = 58 * 1024 * 1024

PROMPT_TILE = 512
STAGE_SHAPE = (8, 128, 1024)
FF_CHUNK = 256

F32 = jnp.float32
BF16 = jnp.bfloat16


def _rms(x, g):
    ms = jnp.mean(x * x, axis=-1, keepdims=True)
    return x * lax.rsqrt(ms + RMS_EPS) * g


def _dot(a, b):
    return jnp.dot(a, b, preferred_element_type=F32)


def _t5_bucket(rel):
    half = N_BUCKETS // 2
    max_exact = half // 2
    ret = jnp.where(rel > 0, half, 0)
    n = jnp.abs(rel)
    nf = jnp.maximum(n, 1).astype(jnp.float32)
    large = max_exact + (jnp.log(nf / max_exact) / math.log(MAX_DISTANCE / max_exact)
                         * (half - max_exact)).astype(jnp.int32)
    large = jnp.minimum(large, half - 1)
    return ret + jnp.where(n < max_exact, n, large)


def _lane_low(shape):
    return lax.broadcasted_iota(jnp.int32, shape, len(shape) - 1) < HEAD_DIM


def _rel_bias(q_pos0, lq, k_pos0, lk, table):
    n = lq + lk - 1
    rel = (k_pos0 - (q_pos0 + lq - 1)) + jnp.arange(n, dtype=jnp.int32)
    f = jnp.transpose(table[_t5_bucket(rel)].astype(F32))
    fp = jnp.pad(f, ((0, 0), (0, 1)))
    skew = jnp.tile(fp, (1, lq))[:, :lq * n].reshape(N_Q_HEADS, lq, n)
    return skew[:, :, lq - 1:].reshape(N_KV_HEADS, GQA_GROUP * lq, lk)


def _sink_col(sink_ref, j, lq):
    blk = lax.broadcasted_iota(jnp.int32, (GQA_GROUP * lq, 1), 0) // lq
    col = jnp.zeros((GQA_GROUP * lq, 1), F32)
    for hl in range(GQA_GROUP):
        col = jnp.where(blk == hl, sink_ref[j * GQA_GROUP + hl], col)
    return col


def _dup_halves(x):
    xr = pltpu.roll(x, HEAD_DIM, axis=x.ndim - 1)
    lo = _lane_low(x.shape)
    return jnp.where(lo, x, xr), jnp.where(lo, xr, x)


def _stack_queries(q_s, j, nb, lq):
    blocks = []
    for pp in range(2):
        c0 = (2 * j + pp) * LANES
        qp = q_s[:, c0:c0 + LANES].reshape(nb, lq, LANES)
        lo = _lane_low(qp.shape)
        zero = jnp.zeros_like(qp)
        blocks.append(jnp.where(lo, qp, zero))
        blocks.append(jnp.where(lo, zero, qp))
    return jnp.concatenate(blocks, axis=1)


def _softmax(s, sink):
    m = jnp.maximum(jnp.max(s, axis=-1, keepdims=True), sink)
    p = jnp.exp(s - m)
    den = jnp.sum(p, axis=-1, keepdims=True) + jnp.exp(sink - m)
    return p.astype(BF16), den


def _softmax_pv(s, sink, vs):
    p, den = _softmax(s, sink)
    return jnp.einsum('bqk,bkd->bqd', p, vs, preferred_element_type=F32) / den


def _unstack_heads(o, lq):
    nb = o.shape[0]
    lo = _lane_low((nb, lq, LANES))
    pairs = []
    for pp in range(2):
        r0 = 2 * pp * lq
        pr = jnp.where(lo, o[:, r0:r0 + lq], o[:, r0 + lq:r0 + 2 * lq])
        pairs.append(pr.reshape(nb * lq, LANES))
    return pairs


def _pool_group(e, g, cnt, halo):
    s = e
    for k in (1, 2, 4, 8)[:g + 1]:
        s = s + pltpu.roll(s, k, axis=0)
    return s[halo:] / cnt - e[halo:]


def _load_weights_bf16(pairs, stage, sem):
    n_slots, srows, scols = stage.shape
    jobs = []
    for src, dst in pairs:
        rows, cols = src.shape
        for r0 in range(0, rows, srows):
            for c0 in range(0, cols, scols):
                jobs.append((src, dst, r0, min(srows, rows - r0), c0, min(scols, cols - c0)))

    def copy(i):
        src, _, r0, nr, c0, ncol = jobs[i]
        slot = i % n_slots
        return pltpu.make_async_copy(src.at[pl.ds(r0, nr), pl.ds(c0, ncol)],
                                     stage.at[slot, pl.ds(0, nr), pl.ds(0, ncol)], sem.at[slot])

    for i in range(min(n_slots - 1, len(jobs))):
        copy(i).start()
    for i, (_, dst, r0, nr, c0, ncol) in enumerate(jobs):
        if i + n_slots - 1 < len(jobs):
            copy(i + n_slots - 1).start()
        copy(i).wait()
        dst[r0:r0 + nr, c0:c0 + ncol] = stage[i % n_slots, :nr, :ncol].astype(BF16)


def _layer_prompt_kernel(sink_ref, bias_ref, x_ref, pe_ref, pe_smp_ref, g1_ref, pscale_ref,
                         g2_ref, g3_ref, g4_ref, x1_smp_hbm,
                         win_hbm, wpool_hbm, wout_hbm, wg_hbm, wu_hbm, wd_hbm, wple_hbm, wpg_hbm,
                         out_ref, nk_ref, nv_ref, npool_ref, out_smp_hbm,
                         q_s, k2_s, v2_s, ue_s, mix_s, x1_s, act_s,
                         win_ref, wpool_ref, wout_ref, wg_ref, wu_ref, wd_ref, wple_ref, wpg_ref,
                         stage_s, sem, *, nt):
    s = pl.program_id(0)
    last = pl.num_programs(0) - 2
    t = lax.rem(jnp.minimum(s, last), nt)
    slot_w = lax.rem(s, 2)
    tile = x_ref.shape[0]
    nc = tile // CHUNK
    nk = (WIN_CHUNKS + 1) * CHUNK

    @pl.when(s == 0)
    def _():
        pairs = [(win_hbm, win_ref), (wout_hbm, wout_ref), (wg_hbm, wg_ref), (wu_hbm, wu_ref),
                 (wd_hbm, wd_ref), (wple_hbm, wple_ref), (wpg_hbm, wpg_ref)]
        pairs += [(wpool_hbm.at[g], wpool_ref.at[g]) for g in range(len(POOL_WINDOWS))]
        _load_weights_bf16(pairs, stage_s, sem)
        pltpu.sync_copy(x1_smp_hbm, x1_s.at[1])
        k2_s[...] = jnp.zeros(k2_s.shape, BF16)
        v2_s[...] = jnp.zeros(v2_s.shape, BF16)
        ue_s[...] = jnp.zeros(ue_s.shape, F32)

    x1 = x1_s[1 - slot_w]
    pe_in = jnp.where(s == 0, pe_smp_ref[...], pe_ref[...])
    d_ff = wg_ref.shape[1]
    ffn = {}
    mixer = {}

    def ffn_norm():
        ffn['h'] = _rms(x1, g3_ref[...]).astype(BF16)

    def ffn_chunk(c):
        c0 = c * FF_CHUNK
        gate = _dot(ffn['h'], wg_ref[:, c0:c0 + FF_CHUNK])
        up = _dot(ffn['h'], wu_ref[:, c0:c0 + FF_CHUNK])
        act_s[:, c0:c0 + FF_CHUNK] = (gate * jax.nn.sigmoid(gate) * up).astype(BF16)

    def ffn_down():
        f = _dot(act_s[...], wd_ref[...])
        ffn['x2'] = x1 + _rms(f, g4_ref[...])

    def ffn_ple():
        x2 = ffn['x2']
        gate2 = jax.nn.sigmoid(_dot(x2.astype(BF16), wpg_ref[...]))
        pe = _dot(pe_in.astype(BF16), wple_ref[...])
        out_ref[...] = x2 + gate2 * pe

    def mixer_in():
        keep = t > 0
        for ref, n_keep in ((k2_s, WINDOW), (v2_s, WINDOW)):
            prev = ref[:, tile:tile + n_keep, :]
            ref[:, :n_keep, :] = jnp.where(keep, prev, jnp.zeros_like(prev))
        prev = ue_s[tile:tile + POOL_HALO, :]
        ue_s[:POOL_HALO, :] = jnp.where(keep, prev, jnp.zeros_like(prev))

        h = _rms(x_ref[...], g1_ref[...]).astype(BF16)
        q_s[...] = (_dot(h, win_ref[:, :ATTN_WIDTH]) * (HEAD_DIM ** -0.5)).astype(BF16)
        zkv = _dot(h, win_ref[:, ATTN_WIDTH:ATTN_WIDTH + 2 * KV_WIDTH])
        k = zkv[:, :KV_WIDTH]
        v = zkv[:, KV_WIDTH:]
        zu = _dot(h, win_ref[:, ATTN_WIDTH + 2 * KV_WIDTH:])
        ue_s[POOL_HALO:, :] = zu
        nk_ref[...] = k[tile - WINDOW:]
        nv_ref[...] = v[tile - WINDOW:]
        npool_ref[...] = zu[tile - POOL_HALO:]
        for src, dst in ((k, k2_s), (v, v2_s)):
            d0, d1 = _dup_halves(src)
            dst[0, WINDOW:, :] = d0.astype(BF16)
            dst[1, WINDOW:, :] = d1.astype(BF16)

    def attn_scores(j):
        cidx = lax.broadcasted_iota(jnp.int32, (nc, 1, nk), 0)
        sidx = lax.broadcasted_iota(jnp.int32, (nc, 1, nk), 2)
        valid = ((t * nc + cidx) * CHUNK + sidx - WIN_CHUNKS * CHUNK) >= 0
        qs = _stack_queries(q_s, j, nc, CHUNK)
        ks = jnp.stack([k2_s[j, c * CHUNK:c * CHUNK + nk, :] for c in range(nc)])
        sc = jnp.einsum('bqd,bkd->bqk', qs, ks, preferred_element_type=F32)
        sc = sc + bias_ref[j][None]
        sc = jnp.where(valid, sc, MASK_VALUE)
        mixer['p', j] = _softmax(sc, _sink_col(sink_ref, j, CHUNK)[None])

    def attn_values(j):
        p, den = mixer.pop(('p', j))
        vs = jnp.stack([v2_s[j, c * CHUNK:c * CHUNK + nk, :] for c in range(nc)])
        o = jnp.einsum('bqk,bkd->bqd', p, vs, preferred_element_type=F32) / den
        for pp, pr in enumerate(_unstack_heads(o, CHUNK)):
            c0 = (2 * j + pp) * LANES
            mix_s[:, c0:c0 + LANES] = pr.astype(BF16)

    def pool():
        pos = t * tile + lax.broadcasted_iota(jnp.int32, (tile, 1), 0)
        for g, w in enumerate(POOL_WINDOWS):
            c0 = g * POOL_GROUP_WIDTH
            cnt = jnp.minimum(pos + 1, w).astype(F32)
            diff = _pool_group(ue_s[:, c0:c0 + POOL_GROUP_WIDTH], g, cnt, POOL_HALO)
            po = _dot(diff.astype(BF16), wpool_ref[g]) * pscale_ref[:, c0:c0 + POOL_GROUP_WIDTH]
            mix_s[:, ATTN_WIDTH + c0:ATTN_WIDTH + c0 + POOL_GROUP_WIDTH] = po.astype(BF16)

    def mixer_out():
        mix = _dot(mix_s[...], wout_ref[...])
        x1_s[slot_w] = x_ref[...] + _rms(mix, g2_ref[...])

    n_ff = d_ff // FF_CHUNK
    order = [ffn_norm, mixer_in, (ffn_chunk, 0), (ffn_chunk, 1), (attn_scores, 0),
             (ffn_chunk, 2), (ffn_chunk, 3), (ffn_chunk, 4), (attn_values, 0), (attn_scores, 1),
             (ffn_chunk, 5), (ffn_chunk, 6), (ffn_chunk, 7), (attn_values, 1), pool]
    order += [(ffn_chunk, c) for c in range(8, n_ff)]
    order += [ffn_down, mixer_out, ffn_ple]
    for step in order:
        if isinstance(step, tuple):
            step[0](step[1])
        else:
            step()

    @pl.when(s == 0)
    def _():
        pltpu.sync_copy(out_ref, out_smp_hbm)


def _mixer_sample_kernel(sink_ref, bias_ref, x_ref, ck_ref, cv_ref, st_ref, g1_ref,
                         win_ref, wpool_ref, pscale_ref, wout_ref, g2_ref,
                         x1_ref, nk_ref, nv_ref, nu_ref,
                         q_s, ue_s, mix_s, *, nb, lq, past_len):
    rows = nb * lq
    ext = POOL_HALO + lq
    x = x_ref[...]
    h = _rms(x, g1_ref[...]).astype(BF16)
    win = win_ref[...].astype(BF16)
    q_s[...] = (_dot(h, win[:, :ATTN_WIDTH]) * (HEAD_DIM ** -0.5)).astype(BF16)
    zkv = _dot(h, win[:, ATTN_WIDTH:ATTN_WIDTH + 2 * KV_WIDTH])
    k = zkv[:, :KV_WIDTH]
    v = zkv[:, KV_WIDTH:]
    zu = _dot(h, win[:, ATTN_WIDTH + 2 * KV_WIDTH:])
    nk_ref[...] = k
    nv_ref[...] = v
    nu_ref[...] = zu

    n_past = ck_ref.shape[1]
    kc = _dup_halves(ck_ref[...].reshape(nb * n_past, LANES))
    vc = _dup_halves(cv_ref[...].reshape(nb * n_past, LANES))
    kn = _dup_halves(k)
    vn = _dup_halves(v)
    for j in range(N_KV_HEADS):
        qs = _stack_queries(q_s, j, nb, lq)
        ks = jnp.concatenate([kc[j].astype(BF16).reshape(nb, n_past, LANES),
                              kn[j].astype(BF16).reshape(nb, lq, LANES)], axis=1)
        vs = jnp.concatenate([vc[j].astype(BF16).reshape(nb, n_past, LANES),
                              vn[j].astype(BF16).reshape(nb, lq, LANES)], axis=1)
        s = jnp.einsum('bqd,bkd->bqk', qs, ks, preferred_element_type=F32)
        s = s + bias_ref[j][None]
        o = _softmax_pv(s, _sink_col(sink_ref, j, lq)[None], vs)
        for pp, pr in enumerate(_unstack_heads(o, lq)):
            c0 = (2 * j + pp) * LANES
            mix_s[:, c0:c0 + LANES] = pr.astype(BF16)

    ue_s[...] = jnp.zeros(ue_s.shape, F32)
    for i in range(nb):
        ue_s[i * ext + 1:i * ext + POOL_HALO, :] = st_ref[i]
        ue_s[i * ext + POOL_HALO:(i + 1) * ext, :] = zu[i * lq:(i + 1) * lq, :]
    pos = past_len + lax.broadcasted_iota(jnp.int32, (nb, lq, 1), 1)
    for g, w in enumerate(POOL_WINDOWS):
        c0 = g * POOL_GROUP_WIDTH
        s = ue_s[:, c0:c0 + POOL_GROUP_WIDTH]
        for sh in (1, 2, 4, 8)[:g + 1]:
            s = s + pltpu.roll(s, sh, axis=0)
        s3 = s.reshape(nb, ext, POOL_GROUP_WIDTH)[:, POOL_HALO:, :]
        u3 = zu[:, c0:c0 + POOL_GROUP_WIDTH].reshape(nb, lq, POOL_GROUP_WIDTH)
        cnt = jnp.minimum(pos + 1, w).astype(F32)
        diff = (s3 / cnt - u3).reshape(rows, POOL_GROUP_WIDTH)
        po = _dot(diff.astype(BF16), wpool_ref[g].astype(BF16)) * pscale_ref[:, c0:c0 + POOL_GROUP_WIDTH]
        mix_s[:, ATTN_WIDTH + c0:ATTN_WIDTH + c0 + POOL_GROUP_WIDTH] = po.astype(BF16)

    mix = _dot(mix_s[...], wout_ref[...].astype(BF16))
    x1_ref[...] = x + _rms(mix, g2_ref[...])


def _const_spec(shape):
    nd = len(shape)
    return pl.BlockSpec(shape, lambda *_: (0,) * nd, pipeline_mode=pl.Buffered(1))


_SMEM_SPEC = pl.BlockSpec(memory_space=pltpu.SMEM)


def _layer_prompt(x, pe, pe_smp, x1_smp, sinks, bias, g1, pscale, g2, g3, g4, weights):
    bsz, seq, d = x.shape
    tile = PROMPT_TILE
    nt = seq // tile
    n_tiles = bsz * nt
    nk = (WIN_CHUNKS + 1) * CHUNK
    assert x1_smp.shape == (tile, d) and pe_smp.shape == (tile, pe.shape[1])
    mixer_tile = lambda s: jnp.minimum(s, n_tiles - 1)
    ffn_tile = lambda s: jnp.maximum(s - 1, 0)
    x_spec = pl.BlockSpec((None, tile, d), lambda s: (mixer_tile(s) // nt, mixer_tile(s) % nt, 0))
    tail = lambda r, w: pl.BlockSpec((None, r, w), lambda s: (mixer_tile(s) // nt, 0, 0))
    ffn_rows = lambda w: pl.BlockSpec((tile, w), lambda s: (ffn_tile(s), 0))
    consts = (pe_smp, g1, pscale, g2, g3, g4)
    hbm_spec = pl.BlockSpec(memory_space=pl.ANY)
    return pl.pallas_call(
        functools.partial(_layer_prompt_kernel, nt=nt),
        grid=(n_tiles + 1,),
        in_specs=[_SMEM_SPEC, _const_spec(bias.shape), x_spec, ffn_rows(pe.shape[1])]
                 + [_const_spec(c.shape) for c in consts] + [hbm_spec] * (1 + len(weights)),
        out_specs=[ffn_rows(d), tail(WINDOW, KV_WIDTH), tail(WINDOW, KV_WIDTH),
                   tail(POOL_HALO, POOL_WIDTH), hbm_spec],
        out_shape=[jax.ShapeDtypeStruct((bsz * seq, d), F32),
                   jax.ShapeDtypeStruct((bsz, WINDOW, KV_WIDTH), F32),
                   jax.ShapeDtypeStruct((bsz, WINDOW, KV_WIDTH), F32),
                   jax.ShapeDtypeStruct((bsz, POOL_HALO, POOL_WIDTH), F32),
                   jax.ShapeDtypeStruct((tile, d), F32)],
        scratch_shapes=[pltpu.VMEM((tile, ATTN_WIDTH), BF16),
                        pltpu.VMEM((N_KV_HEADS, tile + WINDOW, LANES), BF16),
                        pltpu.VMEM((N_KV_HEADS, tile + WINDOW, LANES), BF16),
                        pltpu.VMEM((tile + POOL_HALO, POOL_WIDTH), F32),
                        pltpu.VMEM((tile, MIX_WIDTH), BF16),
                        pltpu.VMEM((2, tile, d), F32),
                        pltpu.VMEM((tile, weights[3].shape[1]), BF16)]
                       + [pltpu.VMEM(w.shape, BF16) for w in weights]
                       + [pltpu.VMEM(STAGE_SHAPE, F32), pltpu.SemaphoreType.DMA((STAGE_SHAPE[0],))],
        compiler_params=pltpu.CompilerParams(dimension_semantics=("arbitrary",),
                                             vmem_limit_bytes=VMEM_LIMIT_BYTES),
        name="layer_prompt",
    )(sinks, bias, x, pe, *consts, x1_smp, *weights)


def _mixer_sample(x, ck, cv, st, sinks, bias, g1, win, wpool, pscale, wout, g2):
    nb, lq, d = x.shape
    rows = nb * lq
    n_past = ck.shape[1]
    full = lambda shape: pl.BlockSpec(shape, lambda i: (0,) * len(shape))
    kern = functools.partial(_mixer_sample_kernel, nb=nb, lq=lq, past_len=PAST_LEN)
    args = (sinks, bias, x.reshape(rows, d), ck, cv, st, g1, win, wpool, pscale, wout, g2)
    return pl.pallas_call(
        kern,
        grid=(1,),
        in_specs=[_SMEM_SPEC] + [full(a.shape) for a in args[1:]],
        out_specs=[full((rows, d)), full((rows, KV_WIDTH)), full((rows, KV_WIDTH)), full((rows, POOL_WIDTH))],
        out_shape=[jax.ShapeDtypeStruct((rows, d), F32),
                   jax.ShapeDtypeStruct((rows, KV_WIDTH), F32),
                   jax.ShapeDtypeStruct((rows, KV_WIDTH), F32),
                   jax.ShapeDtypeStruct((rows, POOL_WIDTH), F32)],
        scratch_shapes=[pltpu.VMEM((rows, ATTN_WIDTH), BF16),
                        pltpu.VMEM((nb * (POOL_HALO + lq), POOL_WIDTH), F32),
                        pltpu.VMEM((rows, MIX_WIDTH), BF16)],
        compiler_params=pltpu.CompilerParams(dimension_semantics=("arbitrary",),
                                             vmem_limit_bytes=VMEM_LIMIT_BYTES),
        name="mixer_sample",
    )(*args)


def kernel(x_prompt, x_sample, cache_k, cache_v, state_pool, p_prompt, p_sample, rel_bias_table,
           g_mix_pre, w_in, attn_sinks, w_pool, pool_scale, w_out, g_mix_post, g_ffn_pre,
           w_ffn_gate, w_ffn_up, w_ffn_down, g_ffn_post, w_ple, w_ple_gate):
    depth = w_in.shape[0]
    assert depth == 1, "single-layer trunk"
    bsz, seq, d = x_prompt.shape
    nb, lq, _ = x_sample.shape
    n_past = cache_k.shape[2]
    assert seq % PROMPT_TILE == 0 and seq >= WINDOW and PAST_LEN - n_past >= 0
    assert nb * lq == PROMPT_TILE and lq >= POOL_PAD

    i = 0
    row = lambda g: g[i].reshape(1, -1)
    weights = (w_in[i], w_pool[i], w_out[i], w_ffn_gate[i], w_ffn_up[i], w_ffn_down[i], w_ple[i],
               w_ple_gate[i])
    g1, g2, g3, g4 = row(g_mix_pre), row(g_mix_post), row(g_ffn_pre), row(g_ffn_post)
    pscale = row(pool_scale)
    sinks = attn_sinks[i]

    nk = (WIN_CHUNKS + 1) * CHUNK
    bias_prompt = _rel_bias(0, CHUNK, -WIN_CHUNKS * CHUNK, nk, rel_bias_table)
    bias_sample = _rel_bias(PAST_LEN, lq, PAST_LEN - n_past, n_past + lq, rel_bias_table)

    ck = cache_k[i].reshape(nb, n_past, KV_WIDTH)
    cv = cache_v[i].reshape(nb, n_past, KV_WIDTH)
    x1s, nks, nvs, nus = _mixer_sample(x_sample, ck, cv, state_pool[i], sinks, bias_sample, g1,
                                       weights[0], weights[1], pscale, weights[2], g2)

    yp, nkp, nvp, nup, ys = _layer_prompt(x_prompt, p_prompt[i].reshape(bsz * seq, PLE_DIM),
                                          p_sample[i].reshape(nb * lq, PLE_DIM), x1s, sinks,
                                          bias_prompt, g1, pscale, g2, g3, g4, weights)
    yp = yp.reshape(bsz, seq, d)
    ys = ys.reshape(nb, lq, d)

    new_k_prompt = nkp.reshape(1, bsz, WINDOW, N_KV_HEADS, HEAD_DIM)
    new_v_prompt = nvp.reshape(1, bsz, WINDOW, N_KV_HEADS, HEAD_DIM)
    new_pool_prompt = nup[:, POOL_HALO - POOL_PAD:, :][None]
    new_k_sample = nks.reshape(1, nb, lq, N_KV_HEADS, HEAD_DIM)
    new_v_sample = nvs.reshape(1, nb, lq, N_KV_HEADS, HEAD_DIM)
    new_pool_sample = nus.reshape(nb, lq, POOL_WIDTH)[:, lq - POOL_PAD:, :][None]
    return (yp, ys, new_k_prompt, new_v_prompt, new_pool_prompt, new_k_sample, new_v_sample,
            new_pool_sample)
```

```python
import functools
import math

import jax
import jax.numpy as jnp
from jax import lax
from jax.experimental import pallas as pl
from jax.experimental.pallas import tpu as pltpu

D_MODEL = 1024
CHUNK = 64
HEAD_DIM = 64
N_Q_HEADS = 8
N_KV_HEADS = 2
GQA_GROUP = N_Q_HEADS // N_KV_HEADS
WINDOW = 128
WIN_CHUNKS = WINDOW // CHUNK
ATTN_WIDTH = N_Q_HEADS * HEAD_DIM
KV_WIDTH = N_KV_HEADS * HEAD_DIM
POOL_WINDOWS = (2, 4, 8, 16)
POOL_WIDTH = D_MODEL // 2
POOL_GROUP_WIDTH = POOL_WIDTH // len(POOL_WINDOWS)
POOL_PAD = max(POOL_WINDOWS) - 1
POOL_HALO = POOL_PAD + 1
MIX_WIDTH = ATTN_WIDTH + POOL_WIDTH
IN_WIDTH = ATTN_WIDTH + 2 * KV_WIDTH + POOL_WIDTH
PLE_DIM = 256
N_BUCKETS = 32
MAX_DISTANCE = 128
RMS_EPS = 1e-6
MASK_VALUE = -1e30
PAST_LEN = 1024

LANES = 128
MXU_COLS = 256
LOG2E = math.log2(math.e)
VMEM_LIMIT_BYTES = 58 * 1024 * 1024

PROMPT_TILE = 512
STAGE_SHAPE = (8, 128, 1024)
FF_CHUNK = 256

F32 = jnp.float32
BF16 = jnp.bfloat16


def _rms(x, g):
    ms = jnp.mean(x * x, axis=-1, keepdims=True)
    return x * lax.rsqrt(ms + RMS_EPS) * g


def _dot(a, b):
    return jnp.dot(a, b, preferred_element_type=F32)


def _t5_bucket(rel):
    half = N_BUCKETS // 2
    max_exact = half // 2
    ret = jnp.where(rel > 0, half, 0)
    n = jnp.abs(rel)
    nf = jnp.maximum(n, 1).astype(jnp.float32)
    large = max_exact + (jnp.log(nf / max_exact) / math.log(MAX_DISTANCE / max_exact)
                         * (half - max_exact)).astype(jnp.int32)
    large = jnp.minimum(large, half - 1)
    return ret + jnp.where(n < max_exact, n, large)


def _pad_keys(nk):
    return -(-nk // MXU_COLS) * MXU_COLS


def _lane_low(shape):
    return lax.broadcasted_iota(jnp.int32, shape, len(shape) - 1) < HEAD_DIM


def _rel_bias(q_pos0, lq, k_pos0, lk, lk_pad, table):
    n = lq + lk - 1
    rel = (k_pos0 - (q_pos0 + lq - 1)) + jnp.arange(n, dtype=jnp.int32)
    f = jnp.transpose(table[_t5_bucket(rel)].astype(F32))
    fp = jnp.pad(f, ((0, 0), (0, 1)))
    skew = jnp.tile(fp, (1, lq))[:, :lq * n].reshape(N_Q_HEADS, lq, n)
    bias = skew[:, :, lq - 1:].reshape(N_KV_HEADS, GQA_GROUP * lq, lk) * LOG2E
    return jnp.pad(bias, ((0, 0), (0, 0), (0, lk_pad - lk)), constant_values=MASK_VALUE)


def _sink_col(sink_ref, j, lq):
    blk = lax.broadcasted_iota(jnp.int32, (GQA_GROUP * lq, 1), 0) // lq
    col = jnp.zeros((GQA_GROUP * lq, 1), F32)
    for hl in range(GQA_GROUP):
        col = jnp.where(blk == hl, sink_ref[j * GQA_GROUP + hl] * LOG2E, col)
    return col


def _dup_halves(x):
    xr = pltpu.roll(x, HEAD_DIM, axis=x.ndim - 1)
    lo = _lane_low(x.shape)
    return jnp.where(lo, x, xr), jnp.where(lo, xr, x)


def _stack_queries(q_s, j, nb, lq):
    blocks = []
    for pp in range(2):
        c0 = (2 * j + pp) * LANES
        qp = q_s[:, c0:c0 + LANES].reshape(nb, lq, LANES)
        lo = _lane_low(qp.shape)
        zero = jnp.zeros_like(qp)
        blocks.append(jnp.where(lo, qp, zero))
        blocks.append(jnp.where(lo, zero, qp))
    return jnp.concatenate(blocks, axis=1)


def _softmax(s, sink):
    m = jnp.maximum(jnp.max(s, axis=-1, keepdims=True), sink)
    p = jnp.exp2(s - m)
    den = jnp.sum(p, axis=-1, keepdims=True) + jnp.exp2(sink - m)
    return p.astype(BF16), den


def _softmax_pv(s, sink, vs):
    p, den = _softmax(s, sink)
    return jnp.einsum('bqk,bkd->bqd', p, vs, preferred_element_type=F32) / den


def _unstack_heads(o, lq):
    nb = o.shape[0]
    lo = _lane_low((nb, lq, LANES))
    pairs = []
    for pp in range(2):
        r0 = 2 * pp * lq
        pr = jnp.where(lo, o[:, r0:r0 + lq], o[:, r0 + lq:r0 + 2 * lq])
        pairs.append(pr.reshape(nb * lq, LANES))
    return pairs


def _pool_group(e, g, cnt, halo):
    s = e
    for k in (1, 2, 4, 8)[:g + 1]:
        s = s + pltpu.roll(s, k, axis=0)
    return s[halo:] / cnt - e[halo:]


def _load_weights_bf16(pairs, stage, sem):
    n_slots, srows, scols = stage.shape
    jobs = []
    for src, dst in pairs:
        rows, cols = src.shape
        for r0 in range(0, rows, srows):
            for c0 in range(0, cols, scols):
                jobs.append((src, dst, r0, min(srows, rows - r0), c0, min(scols, cols - c0)))

    def copy(i):
        src, _, r0, nr, c0, ncol = jobs[i]
        slot = i % n_slots
        return pltpu.make_async_copy(src.at[pl.ds(r0, nr), pl.ds(c0, ncol)],
                                     stage.at[slot, pl.ds(0, nr), pl.ds(0, ncol)], sem.at[slot])

    for i in range(min(n_slots - 1, len(jobs))):
        copy(i).start()
    for i, (_, dst, r0, nr, c0, ncol) in enumerate(jobs):
        if i + n_slots - 1 < len(jobs):
            copy(i + n_slots - 1).start()
        copy(i).wait()
        dst[r0:r0 + nr, c0:c0 + ncol] = stage[i % n_slots, :nr, :ncol].astype(BF16)


def _layer_prompt_kernel(sink_ref, bias_ref, x_ref, pe_ref, pe_smp_ref, g1_ref, pscale_ref,
                         g2_ref, g3_ref, g4_ref, x1_smp_hbm,
                         win_hbm, wpool_hbm, wout_hbm, wg_hbm, wu_hbm, wd_hbm, wple_hbm, wpg_hbm,
                         out_ref, nk_ref, nv_ref, npool_ref, out_smp_hbm,
                         q_s, k2_s, v2_s, ue_s, mix_s, x1_s, act_s,
                         win_ref, wpool_ref, wout_ref, wg_ref, wu_ref, wd_ref, wple_ref, wpg_ref,
                         stage_s, sem, *, nt):
    s = pl.program_id(0)
    last = pl.num_programs(0) - 2
    t = lax.rem(jnp.minimum(s, last), nt)
    slot_w = lax.rem(s, 2)
    tile = x_ref.shape[0]
    nc = tile // CHUNK
    nkp = _pad_keys((WIN_CHUNKS + 1) * CHUNK)

    @pl.when(s == 0)
    def _():
        pairs = [(win_hbm, win_ref), (wout_hbm, wout_ref), (wg_hbm, wg_ref), (wu_hbm, wu_ref),
                 (wd_hbm, wd_ref), (wple_hbm, wple_ref), (wpg_hbm, wpg_ref)]
        pairs += [(wpool_hbm.at[g], wpool_ref.at[g]) for g in range(len(POOL_WINDOWS))]
        _load_weights_bf16(pairs, stage_s, sem)
        pltpu.sync_copy(x1_smp_hbm, x1_s.at[1])
        k2_s[...] = jnp.zeros(k2_s.shape, BF16)
        v2_s[...] = jnp.zeros(v2_s.shape, BF16)
        ue_s[...] = jnp.zeros(ue_s.shape, F32)

    x1 = x1_s[1 - slot_w]
    pe_in = jnp.where(s == 0, pe_smp_ref[...], pe_ref[...])
    d_ff = wg_ref.shape[1]
    ffn = {}
    mixer = {}

    def ffn_norm():
        ffn['h'] = _rms(x1, g3_ref[...]).astype(BF16)

    def ffn_chunk(c):
        c0 = c * FF_CHUNK
        gate = _dot(ffn['h'], wg_ref[:, c0:c0 + FF_CHUNK])
        up = _dot(ffn['h'], wu_ref[:, c0:c0 + FF_CHUNK])
        act_s[:, c0:c0 + FF_CHUNK] = (gate * jax.nn.sigmoid(gate) * up).astype(BF16)

    def ffn_down():
        f = _dot(act_s[...], wd_ref[...])
        ffn['x2'] = x1 + _rms(f, g4_ref[...])

    def ffn_ple():
        x2 = ffn['x2']
        gate2 = jax.nn.sigmoid(_dot(x2.astype(BF16), wpg_ref[...]))
        pe = _dot(pe_in.astype(BF16), wple_ref[...])
        out_ref[...] = x2 + gate2 * pe

    def mixer_in():
        keep = t > 0
        for ref, n_keep in ((k2_s, WINDOW), (v2_s, WINDOW)):
            prev = ref[:, tile:tile + n_keep, :]
            ref[:, :n_keep, :] = jnp.where(keep, prev, jnp.zeros_like(prev))
        prev = ue_s[tile:tile + POOL_HALO, :]
        ue_s[:POOL_HALO, :] = jnp.where(keep, prev, jnp.zeros_like(prev))

        h = _rms(x_ref[...], g1_ref[...]).astype(BF16)
        q_s[...] = (_dot(h, win_ref[:, :ATTN_WIDTH]) * (HEAD_DIM ** -0.5 * LOG2E)).astype(BF16)
        zkv = _dot(h, win_ref[:, ATTN_WIDTH:ATTN_WIDTH + 2 * KV_WIDTH])
        k = zkv[:, :KV_WIDTH]
        v = zkv[:, KV_WIDTH:]
        zu = _dot(h, win_ref[:, ATTN_WIDTH + 2 * KV_WIDTH:])
        ue_s[POOL_HALO:, :] = zu
        nk_ref[...] = k[tile - WINDOW:]
        nv_ref[...] = v[tile - WINDOW:]
        npool_ref[...] = zu[tile - POOL_HALO:]
        for src, dst in ((k, k2_s), (v, v2_s)):
            d0, d1 = _dup_halves(src)
            dst[0, WINDOW:WINDOW + tile, :] = d0.astype(BF16)
            dst[1, WINDOW:WINDOW + tile, :] = d1.astype(BF16)

    def attn_scores(j):
        cidx = lax.broadcasted_iota(jnp.int32, (WIN_CHUNKS, 1, nkp), 0)
        sidx = lax.broadcasted_iota(jnp.int32, (WIN_CHUNKS, 1, nkp), 2)
        valid = ((t * nc + cidx) * CHUNK + sidx - WIN_CHUNKS * CHUNK) >= 0
        qs = _stack_queries(q_s, j, nc, CHUNK)
        ks = jnp.stack([k2_s[j, c * CHUNK:c * CHUNK + nkp, :] for c in range(nc)])
        sc = jnp.einsum('bqd,bkd->bqk', qs, ks, preferred_element_type=F32)
        sc = sc + bias_ref[j][None]
        sc = jnp.concatenate([jnp.where(valid, sc[:WIN_CHUNKS], MASK_VALUE), sc[WIN_CHUNKS:]], axis=0)
        mixer['p', j] = _softmax(sc, _sink_col(sink_ref, j, CHUNK)[None])

    def attn_values(j):
        p, den = mixer.pop(('p', j))
        vs = jnp.stack([v2_s[j, c * CHUNK:c * CHUNK + nkp, :] for c in range(nc)])
        o = jnp.einsum('bqk,bkd->bqd', p, vs, preferred_element_type=F32) / den
        for pp, pr in enumerate(_unstack_heads(o, CHUNK)):
            c0 = (2 * j + pp) * LANES
            mix_s[:, c0:c0 + LANES] = pr.astype(BF16)

    def pool():
        pos = t * tile + lax.broadcasted_iota(jnp.int32, (tile, 1), 0)
        for g, w in enumerate(POOL_WINDOWS):
            c0 = g * POOL_GROUP_WIDTH
            cnt = jnp.minimum(pos + 1, w).astype(F32)
            diff = _pool_group(ue_s[:, c0:c0 + POOL_GROUP_WIDTH], g, cnt, POOL_HALO)
            po = _dot(diff.astype(BF16), wpool_ref[g]) * pscale_ref[:, c0:c0 + POOL_GROUP_WIDTH]
            mix_s[:, ATTN_WIDTH + c0:ATTN_WIDTH + c0 + POOL_GROUP_WIDTH] = po.astype(BF16)

    def mixer_out():
        mix = _dot(mix_s[...], wout_ref[...])
        x1_s[slot_w] = x_ref[...] + _rms(mix, g2_ref[...])

    n_ff = d_ff // FF_CHUNK
    order = [ffn_norm, mixer_in, (ffn_chunk, 0), (ffn_chunk, 1), (attn_scores, 0),
             (ffn_chunk, 2), (ffn_chunk, 3), (ffn_chunk, 4), (attn_values, 0), (attn_scores, 1),
             (ffn_chunk, 5), (ffn_chunk, 6), (ffn_chunk, 7), (attn_values, 1), pool]
    order += [(ffn_chunk, c) for c in range(8, n_ff)]
    order += [ffn_down, mixer_out, ffn_ple]
    for step in order:
        if isinstance(step, tuple):
            step[0](step[1])
        else:
            step()

    @pl.when(s == 0)
    def _():
        pltpu.sync_copy(out_ref, out_smp_hbm)


def _mixer_sample_kernel(sink_ref, bias_ref, x_ref, ck_ref, cv_ref, st_ref, g1_ref,
                         win_ref, wpool_ref, pscale_ref, wout_ref, g2_ref,
                         x1_ref, nk_ref, nv_ref, nu_ref,
                         q_s, ue_s, mix_s, *, nb, lq, past_len):
    rows = nb * lq
    ext = POOL_HALO + lq
    x = x_ref[...]
    h = _rms(x, g1_ref[...]).astype(BF16)
    win = win_ref[...].astype(BF16)
    q_s[...] = (_dot(h, win[:, :ATTN_WIDTH]) * (HEAD_DIM ** -0.5 * LOG2E)).astype(BF16)
    zkv = _dot(h, win[:, ATTN_WIDTH:ATTN_WIDTH + 2 * KV_WIDTH])
    k = zkv[:, :KV_WIDTH]
    v = zkv[:, KV_WIDTH:]
    zu = _dot(h, win[:, ATTN_WIDTH + 2 * KV_WIDTH:])
    nk_ref[...] = k
    nv_ref[...] = v
    nu_ref[...] = zu

    n_past = ck_ref.shape[1]
    kc = _dup_halves(ck_ref[...].reshape(nb * n_past, LANES))
    vc = _dup_halves(cv_ref[...].reshape(nb * n_past, LANES))
    kn = _dup_halves(k)
    vn = _dup_halves(v)
    for j in range(N_KV_HEADS):
        qs = _stack_queries(q_s, j, nb, lq)
        ks = jnp.concatenate([kc[j].astype(BF16).reshape(nb, n_past, LANES),
                              kn[j].astype(BF16).reshape(nb, lq, LANES)], axis=1)
        vs = jnp.concatenate([vc[j].astype(BF16).reshape(nb, n_past, LANES),
                              vn[j].astype(BF16).reshape(nb, lq, LANES)], axis=1)
        s = jnp.einsum('bqd,bkd->bqk', qs, ks, preferred_element_type=F32)
        s = s + bias_ref[j][None]
        o = _softmax_pv(s, _sink_col(sink_ref, j, lq)[None], vs)
        for pp, pr in enumerate(_unstack_heads(o, lq)):
            c0 = (2 * j + pp) * LANES
            mix_s[:, c0:c0 + LANES] = pr.astype(BF16)

    ue_s[...] = jnp.zeros(ue_s.shape, F32)
    for i in range(nb):
        ue_s[i * ext + 1:i * ext + POOL_HALO, :] = st_ref[i]
        ue_s[i * ext + POOL_HALO:(i + 1) * ext, :] = zu[i * lq:(i + 1) * lq, :]
    pos = past_len + lax.broadcasted_iota(jnp.int32, (nb, lq, 1), 1)
    for g, w in enumerate(POOL_WINDOWS):
        c0 = g * POOL_GROUP_WIDTH
        s = ue_s[:, c0:c0 + POOL_GROUP_WIDTH]
        for sh in (1, 2, 4, 8)[:g + 1]:
            s = s + pltpu.roll(s, sh, axis=0)
        s3 = s.reshape(nb, ext, POOL_GROUP_WIDTH)[:, POOL_HALO:, :]
        u3 = zu[:, c0:c0 + POOL_GROUP_WIDTH].reshape(nb, lq, POOL_GROUP_WIDTH)
        cnt = jnp.minimum(pos + 1, w).astype(F32)
        diff = (s3 / cnt - u3).reshape(rows, POOL_GROUP_WIDTH)
        po = _dot(diff.astype(BF16), wpool_ref[g].astype(BF16)) * pscale_ref[:, c0:c0 + POOL_GROUP_WIDTH]
        mix_s[:, ATTN_WIDTH + c0:ATTN_WIDTH + c0 + POOL_GROUP_WIDTH] = po.astype(BF16)

    mix = _dot(mix_s[...], wout_ref[...].astype(BF16))
    x1_ref[...] = x + _rms(mix, g2_ref[...])


def _const_spec(shape):
    nd = len(shape)
    return pl.BlockSpec(shape, lambda *_: (0,) * nd, pipeline_mode=pl.Buffered(1))


_SMEM_SPEC = pl.BlockSpec(memory_space=pltpu.SMEM)


def _layer_prompt(x, pe, pe_smp, x1_smp, sinks, bias, g1, pscale, g2, g3, g4, weights):
    bsz, seq, d = x.shape
    tile = PROMPT_TILE
    nt = seq // tile
    n_tiles = bsz * nt
    nk = (WIN_CHUNKS + 1) * CHUNK
    kv_rows = WINDOW + tile + _pad_keys(nk) - nk
    assert x1_smp.shape == (tile, d) and pe_smp.shape == (tile, pe.shape[1])
    mixer_tile = lambda s: jnp.minimum(s, n_tiles - 1)
    ffn_tile = lambda s: jnp.maximum(s - 1, 0)
    x_spec = pl.BlockSpec((None, tile, d), lambda s: (mixer_tile(s) // nt, mixer_tile(s) % nt, 0))
    tail = lambda r, w: pl.BlockSpec((None, r, w), lambda s: (mixer_tile(s) // nt, 0, 0))
    ffn_rows = lambda w: pl.BlockSpec((tile, w), lambda s: (ffn_tile(s), 0))
    consts = (pe_smp, g1, pscale, g2, g3, g4)
    hbm_spec = pl.BlockSpec(memory_space=pl.ANY)
    return pl.pallas_call(
        functools.partial(_layer_prompt_kernel, nt=nt),
        grid=(n_tiles + 1,),
        in_specs=[_SMEM_SPEC, _const_spec(bias.shape), x_spec, ffn_rows(pe.shape[1])]
                 + [_const_spec(c.shape) for c in consts] + [hbm_spec] * (1 + len(weights)),
        out_specs=[ffn_rows(d), tail(WINDOW, KV_WIDTH), tail(WINDOW, KV_WIDTH),
                   tail(POOL_HALO, POOL_WIDTH), hbm_spec],
        out_shape=[jax.ShapeDtypeStruct((bsz * seq, d), F32),
                   jax.ShapeDtypeStruct((bsz, WINDOW, KV_WIDTH), F32),
                   jax.ShapeDtypeStruct((bsz, WINDOW, KV_WIDTH), F32),
                   jax.ShapeDtypeStruct((bsz, POOL_HALO, POOL_WIDTH), F32),
                   jax.ShapeDtypeStruct((tile, d), F32)],
        scratch_shapes=[pltpu.VMEM((tile, ATTN_WIDTH), BF16),
                        pltpu.VMEM((N_KV_HEADS, kv_rows, LANES), BF16),
                        pltpu.VMEM((N_KV_HEADS, kv_rows, LANES), BF16),
                        pltpu.VMEM((tile + POOL_HALO, POOL_WIDTH), F32),
                        pltpu.VMEM((tile, MIX_WIDTH), BF16),
                        pltpu.VMEM((2, tile, d), F32),
                        pltpu.VMEM((tile, weights[3].shape[1]), BF16)]
                       + [pltpu.VMEM(w.shape, BF16) for w in weights]
                       + [pltpu.VMEM(STAGE_SHAPE, F32), pltpu.SemaphoreType.DMA((STAGE_SHAPE[0],))],
        compiler_params=pltpu.CompilerParams(dimension_semantics=("arbitrary",),
                                             vmem_limit_bytes=VMEM_LIMIT_BYTES),
        name="layer_prompt",
    )(sinks, bias, x, pe, *consts, x1_smp, *weights)


def _mixer_sample(x, ck, cv, st, sinks, bias, g1, win, wpool, pscale, wout, g2):
    nb, lq, d = x.shape
    rows = nb * lq
    n_past = ck.shape[1]
    full = lambda shape: pl.BlockSpec(shape, lambda i: (0,) * len(shape))
    kern = functools.partial(_mixer_sample_kernel, nb=nb, lq=lq, past_len=PAST_LEN)
    args = (sinks, bias, x.reshape(rows, d), ck, cv, st, g1, win, wpool, pscale, wout, g2)
    return pl.pallas_call(
        kern,
        grid=(1,),
        in_specs=[_SMEM_SPEC] + [full(a.shape) for a in args[1:]],
        out_specs=[full((rows, d)), full((rows, KV_WIDTH)), full((rows, KV_WIDTH)), full((rows, POOL_WIDTH))],
        out_shape=[jax.ShapeDtypeStruct((rows, d), F32),
                   jax.ShapeDtypeStruct((rows, KV_WIDTH), F32),
                   jax.ShapeDtypeStruct((rows, KV_WIDTH), F32),
                   jax.ShapeDtypeStruct((rows, POOL_WIDTH), F32)],
        scratch_shapes=[pltpu.VMEM((rows, ATTN_WIDTH), BF16),
                        pltpu.VMEM((nb * (POOL_HALO + lq), POOL_WIDTH), F32),
                        pltpu.VMEM((rows, MIX_WIDTH), BF16)],
        compiler_params=pltpu.CompilerParams(dimension_semantics=("arbitrary",),
                                             vmem_limit_bytes=VMEM_LIMIT_BYTES),
        name="mixer_sample",
    )(*args)


def kernel(x_prompt, x_sample, cache_k, cache_v, state_pool, p_prompt, p_sample, rel_bias_table,
           g_mix_pre, w_in, attn_sinks, w_pool, pool_scale, w_out, g_mix_post, g_ffn_pre,
           w_ffn_gate, w_ffn_up, w_ffn_down, g_ffn_post, w_ple, w_ple_gate):
    depth = w_in.shape[0]
    assert depth == 1, "single-layer trunk"
    bsz, seq, d = x_prompt.shape
    nb, lq, _ = x_sample.shape
    n_past = cache_k.shape[2]
    assert seq % PROMPT_TILE == 0 and seq >= WINDOW and PAST_LEN - n_past >= 0
    assert nb * lq == PROMPT_TILE and lq >= POOL_PAD

    i = 0
    row = lambda g: g[i].reshape(1, -1)
    weights = (w_in[i], w_pool[i], w_out[i], w_ffn_gate[i], w_ffn_up[i], w_ffn_down[i], w_ple[i],
               w_ple_gate[i])
    g1, g2, g3, g4 = row(g_mix_pre), row(g_mix_post), row(g_ffn_pre), row(g_ffn_post)
    pscale = row(pool_scale)
    sinks = attn_sinks[i]

    nk = (WIN_CHUNKS + 1) * CHUNK
    bias_prompt = _rel_bias(0, CHUNK, -WIN_CHUNKS * CHUNK, nk, _pad_keys(nk), rel_bias_table)
    bias_sample = _rel_bias(PAST_LEN, lq, PAST_LEN - n_past, n_past + lq, n_past + lq, rel_bias_table)

    ck = cache_k[i].reshape(nb, n_past, KV_WIDTH)
    cv = cache_v[i].reshape(nb, n_past, KV_WIDTH)
    x1s, nks, nvs, nus = _mixer_sample(x_sample, ck, cv, state_pool[i], sinks, bias_sample, g1,
                                       weights[0], weights[1], pscale, weights[2], g2)

    yp, nkp, nvp, nup, ys = _layer_prompt(x_prompt, p_prompt[i].reshape(bsz * seq, PLE_DIM),
                                          p_sample[i].reshape(nb * lq, PLE_DIM), x1s, sinks,
                                          bias_prompt, g1, pscale, g2, g3, g4, weights)
    yp = yp.reshape(bsz, seq, d)
    ys = ys.reshape(nb, lq, d)

    new_k_prompt = nkp.reshape(1, bsz, WINDOW, N_KV_HEADS, HEAD_DIM)
    new_v_prompt = nvp.reshape(1, bsz, WINDOW, N_KV_HEADS, HEAD_DIM)
    new_pool_prompt = nup[:, POOL_HALO - POOL_PAD:, :][None]
    new_k_sample = nks.reshape(1, nb, lq, N_KV_HEADS, HEAD_DIM)
    new_v_sample = nvs.reshape(1, nb, lq, N_KV_HEADS, HEAD_DIM)
    new_pool_sample = nus.reshape(nb, lq, POOL_WIDTH)[:, lq - POOL_PAD:, :][None]
    return (yp, ys, new_k_prompt, new_v_prompt, new_pool_prompt, new_k_sample, new_v_sample,
            new_pool_sample)
```

```python
import functools
import math

import jax
import jax.numpy as jnp
from jax import lax
from jax.experimental import pallas as pl
from jax.experimental.pallas import tpu as pltpu

D_MODEL = 1024
CHUNK = 64
HEAD_DIM = 64
N_Q_HEADS = 8
N_KV_HEADS = 2
GQA_GROUP = N_Q_HEADS // N_KV_HEADS
WINDOW = 128
WIN_CHUNKS = WINDOW // CHUNK
ATTN_WIDTH = N_Q_HEADS * HEAD_DIM
KV_WIDTH = N_KV_HEADS * HEAD_DIM
POOL_WINDOWS = (2, 4, 8, 16)
POOL_WIDTH = D_MODEL // 2
POOL_GROUP_WIDTH = POOL_WIDTH // len(POOL_WINDOWS)
POOL_PAD = max(POOL_WINDOWS) - 1
POOL_HALO = POOL_PAD + 1
MIX_WIDTH = ATTN_WIDTH + POOL_WIDTH
IN_WIDTH = ATTN_WIDTH + 2 * KV_WIDTH + POOL_WIDTH
PLE_DIM = 256
N_BUCKETS = 32
MAX_DISTANCE = 128
RMS_EPS = 1e-6
MASK_VALUE = -1e30
PAST_LEN = 1024

LANES = 128
MXU_COLS = 256
LOG2E = math.log2(math.e)
VMEM_LIMIT_BYTES = 58 * 1024 * 1024

PROMPT_TILE = 512
STAGE_SHAPE = (8, 128, 1024)
FF_CHUNK = 256
SAMPLE_STEPS = 2

F32 = jnp.float32
BF16 = jnp.bfloat16


def _rms(x, g):
    ms = jnp.mean(x * x, axis=-1, keepdims=True)
    return x * lax.rsqrt(ms + RMS_EPS) * g


def _dot(a, b):
    return jnp.dot(a, b, preferred_element_type=F32)


def _t5_bucket(rel):
    half = N_BUCKETS // 2
    max_exact = half // 2
    ret = jnp.where(rel > 0, half, 0)
    n = jnp.abs(rel)
    nf = jnp.maximum(n, 1).astype(jnp.float32)
    large = max_exact + (jnp.log(nf / max_exact) / math.log(MAX_DISTANCE / max_exact)
                         * (half - max_exact)).astype(jnp.int32)
    large = jnp.minimum(large, half - 1)
    return ret + jnp.where(n < max_exact, n, large)


def _pad_keys(nk):
    return -(-nk // MXU_COLS) * MXU_COLS


def _lane_low(shape):
    return lax.broadcasted_iota(jnp.int32, shape, len(shape) - 1) < HEAD_DIM


def _rel_bias(q_pos0, lq, k_pos0, lk, lk_pad, table):
    n = lq + lk - 1
    rel = (k_pos0 - (q_pos0 + lq - 1)) + jnp.arange(n, dtype=jnp.int32)
    f = jnp.transpose(table[_t5_bucket(rel)].astype(F32))
    fp = jnp.pad(f, ((0, 0), (0, 1)))
    skew = jnp.tile(fp, (1, lq))[:, :lq * n].reshape(N_Q_HEADS, lq, n)
    bias = skew[:, :, lq - 1:].reshape(N_KV_HEADS, GQA_GROUP * lq, lk) * LOG2E
    return jnp.pad(bias, ((0, 0), (0, 0), (0, lk_pad - lk)), constant_values=MASK_VALUE)


def _sink_col(sink_ref, j, lq):
    blk = lax.broadcasted_iota(jnp.int32, (GQA_GROUP * lq, 1), 0) // lq
    col = jnp.zeros((GQA_GROUP * lq, 1), F32)
    for hl in range(GQA_GROUP):
        col = jnp.where(blk == hl, sink_ref[j * GQA_GROUP + hl] * LOG2E, col)
    return col


def _dup_halves(x):
    xr = pltpu.roll(x, HEAD_DIM, axis=x.ndim - 1)
    lo = _lane_low(x.shape)
    return jnp.where(lo, x, xr), jnp.where(lo, xr, x)


def _stack_queries(q_s, j, nb, lq):
    blocks = []
    for pp in range(2):
        c0 = (2 * j + pp) * LANES
        qp = q_s[:, c0:c0 + LANES].reshape(nb, lq, LANES)
        lo = _lane_low(qp.shape)
        zero = jnp.zeros_like(qp)
        blocks.append(jnp.where(lo, qp, zero))
        blocks.append(jnp.where(lo, zero, qp))
    return jnp.concatenate(blocks, axis=1)


def _softmax(s, sink):
    m = jnp.maximum(jnp.max(s, axis=-1, keepdims=True), sink)
    p = jnp.exp2(s - m)
    den = jnp.sum(p, axis=-1, keepdims=True) + jnp.exp2(sink - m)
    return p.astype(BF16), den


def _softmax_pv(s, sink, vs):
    p, den = _softmax(s, sink)
    return jnp.einsum('bqk,bkd->bqd', p, vs, preferred_element_type=F32) / den


def _unstack_heads(o, lq):
    nb = o.shape[0]
    lo = _lane_low((nb, lq, LANES))
    pairs = []
    for pp in range(2):
        r0 = 2 * pp * lq
        pr = jnp.where(lo, o[:, r0:r0 + lq], o[:, r0 + lq:r0 + 2 * lq])
        pairs.append(pr.reshape(nb * lq, LANES))
    return pairs


def _pool_group(e, g, cnt, halo):
    s = e
    for k in (1, 2, 4, 8)[:g + 1]:
        s = s + pltpu.roll(s, k, axis=0)
    return s[halo:] / cnt - e[halo:]


def _load_weights_bf16(pairs, stage, sem):
    n_slots, srows, scols = stage.shape
    jobs = []
    for src, dst in pairs:
        rows, cols = src.shape
        for r0 in range(0, rows, srows):
            for c0 in range(0, cols, scols):
                jobs.append((src, dst, r0, min(srows, rows - r0), c0, min(scols, cols - c0)))

    def copy(i):
        src, _, r0, nr, c0, ncol = jobs[i]
        slot = i % n_slots
        return pltpu.make_async_copy(src.at[pl.ds(r0, nr), pl.ds(c0, ncol)],
                                     stage.at[slot, pl.ds(0, nr), pl.ds(0, ncol)], sem.at[slot])

    for i in range(min(n_slots - 1, len(jobs))):
        copy(i).start()
    for i, (_, dst, r0, nr, c0, ncol) in enumerate(jobs):
        if i + n_slots - 1 < len(jobs):
            copy(i + n_slots - 1).start()
        copy(i).wait()
        dst[r0:r0 + nr, c0:c0 + ncol] = stage[i % n_slots, :nr, :ncol].astype(BF16)


def _layer_prompt_kernel(sink_ref, bias_ref, x_ref, pe_ref, pe_smp_ref, g1_ref, pscale_ref,
                         g2_ref, g3_ref, g4_ref, x1_smp_hbm,
                         win_hbm, wpool_hbm, wout_hbm, wg_hbm, wu_hbm, wd_hbm, wple_hbm, wpg_hbm,
                         out_ref, nk_ref, nv_ref, npool_ref, out_smp_hbm,
                         q_s, k2_s, v2_s, ue_s, mix_s, x1_s, act_s,
                         win_ref, wpool_ref, wout_ref, wg_ref, wu_ref, wd_ref, wple_ref, wpg_ref,
                         stage_s, sem, *, nt):
    s = pl.program_id(0)
    last = pl.num_programs(0) - 2
    t = lax.rem(jnp.minimum(s, last), nt)
    slot_w = lax.rem(s, 2)
    tile = x_ref.shape[0]
    nc = tile // CHUNK
    nkp = _pad_keys((WIN_CHUNKS + 1) * CHUNK)

    @pl.when(s == 0)
    def _():
        pairs = [(win_hbm, win_ref), (wout_hbm, wout_ref), (wg_hbm, wg_ref), (wu_hbm, wu_ref),
                 (wd_hbm, wd_ref), (wple_hbm, wple_ref), (wpg_hbm, wpg_ref)]
        pairs += [(wpool_hbm.at[g], wpool_ref.at[g]) for g in range(len(POOL_WINDOWS))]
        _load_weights_bf16(pairs, stage_s, sem)
        pltpu.sync_copy(x1_smp_hbm, x1_s.at[1])
        k2_s[...] = jnp.zeros(k2_s.shape, BF16)
        v2_s[...] = jnp.zeros(v2_s.shape, BF16)
        ue_s[...] = jnp.zeros(ue_s.shape, F32)

    def run_step():
        x1 = x1_s[1 - slot_w]
        pe_in = jnp.where(s == 0, pe_smp_ref[...], pe_ref[...])
        d_ff = wg_ref.shape[1]
        ffn = {}
        mixer = {}

        def ffn_norm():
            ffn['h'] = _rms(x1, g3_ref[...]).astype(BF16)

        def ffn_chunk(c):
            c0 = c * FF_CHUNK
            gate = _dot(ffn['h'], wg_ref[:, c0:c0 + FF_CHUNK])
            up = _dot(ffn['h'], wu_ref[:, c0:c0 + FF_CHUNK])
            act_s[:, c0:c0 + FF_CHUNK] = (gate * jax.nn.sigmoid(gate) * up).astype(BF16)

        def ffn_down():
            f = _dot(act_s[...], wd_ref[...])
            ffn['x2'] = x1 + _rms(f, g4_ref[...])

        def ffn_ple():
            x2 = ffn['x2']
            gate2 = jax.nn.sigmoid(_dot(x2.astype(BF16), wpg_ref[...]))
            pe = _dot(pe_in.astype(BF16), wple_ref[...])
            out_ref[...] = x2 + gate2 * pe

        def mixer_in():
            keep = t > 0
            for ref, n_keep in ((k2_s, WINDOW), (v2_s, WINDOW)):
                prev = ref[:, tile:tile + n_keep, :]
                ref[:, :n_keep, :] = jnp.where(keep, prev, jnp.zeros_like(prev))
            prev = ue_s[tile:tile + POOL_HALO, :]
            ue_s[:POOL_HALO, :] = jnp.where(keep, prev, jnp.zeros_like(prev))

            h = _rms(x_ref[...], g1_ref[...]).astype(BF16)
            q_s[...] = (_dot(h, win_ref[:, :ATTN_WIDTH]) * (HEAD_DIM ** -0.5 * LOG2E)).astype(BF16)
            zkv = _dot(h, win_ref[:, ATTN_WIDTH:ATTN_WIDTH + 2 * KV_WIDTH])
            k = zkv[:, :KV_WIDTH]
            v = zkv[:, KV_WIDTH:]
            zu = _dot(h, win_ref[:, ATTN_WIDTH + 2 * KV_WIDTH:])
            ue_s[POOL_HALO:, :] = zu
            nk_ref[...] = k[tile - WINDOW:]
            nv_ref[...] = v[tile - WINDOW:]
            npool_ref[...] = zu[tile - POOL_PAD:]
            for src, dst in ((k, k2_s), (v, v2_s)):
                d0, d1 = _dup_halves(src)
                dst[0, WINDOW:WINDOW + tile, :] = d0.astype(BF16)
                dst[1, WINDOW:WINDOW + tile, :] = d1.astype(BF16)

        def attn_scores(j):
            cidx = lax.broadcasted_iota(jnp.int32, (WIN_CHUNKS, 1, nkp), 0)
            sidx = lax.broadcasted_iota(jnp.int32, (WIN_CHUNKS, 1, nkp), 2)
            valid = ((t * nc + cidx) * CHUNK + sidx - WIN_CHUNKS * CHUNK) >= 0
            qs = _stack_queries(q_s, j, nc, CHUNK)
            ks = jnp.stack([k2_s[j, c * CHUNK:c * CHUNK + nkp, :] for c in range(nc)])
            sc = jnp.einsum('bqd,bkd->bqk', qs, ks, preferred_element_type=F32)
            sc = sc + bias_ref[j][None]
            sc = jnp.concatenate([jnp.where(valid, sc[:WIN_CHUNKS], MASK_VALUE), sc[WIN_CHUNKS:]], axis=0)
            mixer['p', j] = _softmax(sc, _sink_col(sink_ref, j, CHUNK)[None])

        def attn_values(j):
            p, den = mixer.pop(('p', j))
            vs = jnp.stack([v2_s[j, c * CHUNK:c * CHUNK + nkp, :] for c in range(nc)])
            o = jnp.einsum('bqk,bkd->bqd', p, vs, preferred_element_type=F32) / den
            for pp, pr in enumerate(_unstack_heads(o, CHUNK)):
                c0 = (2 * j + pp) * LANES
                mix_s[:, c0:c0 + LANES] = pr.astype(BF16)

        def pool():
            pos = t * tile + lax.broadcasted_iota(jnp.int32, (tile, 1), 0)
            for g, w in enumerate(POOL_WINDOWS):
                c0 = g * POOL_GROUP_WIDTH
                cnt = jnp.minimum(pos + 1, w).astype(F32)
                diff = _pool_group(ue_s[:, c0:c0 + POOL_GROUP_WIDTH], g, cnt, POOL_HALO)
                po = _dot(diff.astype(BF16), wpool_ref[g]) * pscale_ref[:, c0:c0 + POOL_GROUP_WIDTH]
                mix_s[:, ATTN_WIDTH + c0:ATTN_WIDTH + c0 + POOL_GROUP_WIDTH] = po.astype(BF16)

        def mixer_out():
            mix = _dot(mix_s[...], wout_ref[...])
            x1_s[slot_w] = x_ref[...] + _rms(mix, g2_ref[...])

        n_ff = d_ff // FF_CHUNK
        order = [ffn_norm, mixer_in, (ffn_chunk, 0), (ffn_chunk, 1), (attn_scores, 0),
                 (ffn_chunk, 2), (ffn_chunk, 3), (ffn_chunk, 4), (attn_values, 0), (attn_scores, 1),
                 (ffn_chunk, 5), (ffn_chunk, 6), (ffn_chunk, 7), (attn_values, 1), pool]
        order += [(ffn_chunk, c) for c in range(8, n_ff)]
        order += [ffn_down, mixer_out, ffn_ple]
        for step in order:
            fn, args = (step[0], step[1:]) if isinstance(step, tuple) else (step, ())
            fn(*args)

    run_step()

    @pl.when(s == 0)
    def _():
        pltpu.sync_copy(out_ref, out_smp_hbm)


def _mixer_sample_kernel(sink_ref, bias_ref, x_ref, ck_ref, cv_ref, st_ref, g1_ref,
                         win_ref, wpool_ref, pscale_ref, wout_ref, g2_ref,
                         x1_ref, nk_ref, nv_ref, nu_ref,
                         q_s, ue_s, mix_s, *, nb, lq, past_len):
    rows = nb * lq
    ext = POOL_HALO + lq
    x = x_ref[...]
    h = _rms(x, g1_ref[...]).astype(BF16)
    win = win_ref[...].astype(BF16)
    q_s[...] = (_dot(h, win[:, :ATTN_WIDTH]) * (HEAD_DIM ** -0.5 * LOG2E)).astype(BF16)
    zkv = _dot(h, win[:, ATTN_WIDTH:ATTN_WIDTH + 2 * KV_WIDTH])
    k = zkv[:, :KV_WIDTH]
    v = zkv[:, KV_WIDTH:]
    zu = _dot(h, win[:, ATTN_WIDTH + 2 * KV_WIDTH:])
    nk_ref[...] = k
    nv_ref[...] = v
    for i in range(nb):
        nu_ref[i] = zu[(i + 1) * lq - POOL_PAD:(i + 1) * lq, :]

    n_past = ck_ref.shape[1]
    kc = _dup_halves(ck_ref[...].reshape(nb * n_past, LANES))
    vc = _dup_halves(cv_ref[...].reshape(nb * n_past, LANES))
    kn = _dup_halves(k)
    vn = _dup_halves(v)
    for j in range(N_KV_HEADS):
        qs = _stack_queries(q_s, j, nb, lq)
        ks = jnp.concatenate([kc[j].astype(BF16).reshape(nb, n_past, LANES),
                              kn[j].astype(BF16).reshape(nb, lq, LANES)], axis=1)
        vs = jnp.concatenate([vc[j].astype(BF16).reshape(nb, n_past, LANES),
                              vn[j].astype(BF16).reshape(nb, lq, LANES)], axis=1)
        s = jnp.einsum('bqd,bkd->bqk', qs, ks, preferred_element_type=F32)
        s = s + bias_ref[j][None]
        o = _softmax_pv(s, _sink_col(sink_ref, j, lq)[None], vs)
        for pp, pr in enumerate(_unstack_heads(o, lq)):
            c0 = (2 * j + pp) * LANES
            mix_s[:, c0:c0 + LANES] = pr.astype(BF16)

    ue_s[...] = jnp.zeros(ue_s.shape, F32)
    for i in range(nb):
        ue_s[i * ext + 1:i * ext + POOL_HALO, :] = st_ref[i]
        ue_s[i * ext + POOL_HALO:(i + 1) * ext, :] = zu[i * lq:(i + 1) * lq, :]
    pos = past_len + lax.broadcasted_iota(jnp.int32, (nb, lq, 1), 1)
    for g, w in enumerate(POOL_WINDOWS):
        c0 = g * POOL_GROUP_WIDTH
        s = ue_s[:, c0:c0 + POOL_GROUP_WIDTH]
        for sh in (1, 2, 4, 8)[:g + 1]:
            s = s + pltpu.roll(s, sh, axis=0)
        s3 = s.reshape(nb, ext, POOL_GROUP_WIDTH)[:, POOL_HALO:, :]
        u3 = zu[:, c0:c0 + POOL_GROUP_WIDTH].reshape(nb, lq, POOL_GROUP_WIDTH)
        cnt = jnp.minimum(pos + 1, w).astype(F32)
        diff = (s3 / cnt - u3).reshape(rows, POOL_GROUP_WIDTH)
        po = _dot(diff.astype(BF16), wpool_ref[g].astype(BF16)) * pscale_ref[:, c0:c0 + POOL_GROUP_WIDTH]
        mix_s[:, ATTN_WIDTH + c0:ATTN_WIDTH + c0 + POOL_GROUP_WIDTH] = po.astype(BF16)

    mix = _dot(mix_s[...], wout_ref[...].astype(BF16))
    x1_ref[...] = x + _rms(mix, g2_ref[...])


def _const_spec(shape):
    nd = len(shape)
    return pl.BlockSpec(shape, lambda *_: (0,) * nd, pipeline_mode=pl.Buffered(1))


_SMEM_SPEC = pl.BlockSpec(memory_space=pltpu.SMEM)


def _layer_prompt(x, pe, pe_smp, x1_smp, sinks, bias, g1, pscale, g2, g3, g4, weights):
    bsz, seq, d = x.shape
    tile = PROMPT_TILE
    nt = seq // tile
    n_tiles = bsz * nt
    nk = (WIN_CHUNKS + 1) * CHUNK
    kv_rows = WINDOW + tile + _pad_keys(nk) - nk
    assert x1_smp.shape == (tile, d) and pe_smp.shape == (tile, pe.shape[1])
    mixer_tile = lambda s: jnp.minimum(s, n_tiles - 1)
    ffn_tile = lambda s: jnp.maximum(s - 1, 0)
    x_spec = pl.BlockSpec((None, tile, d), lambda s: (mixer_tile(s) // nt, mixer_tile(s) % nt, 0))
    tail = lambda r, w: pl.BlockSpec((None, r, w), lambda s: (mixer_tile(s) // nt, 0, 0))
    ffn_rows = lambda w: pl.BlockSpec((tile, w), lambda s: (ffn_tile(s), 0))
    consts = (pe_smp, g1, pscale, g2, g3, g4)
    hbm_spec = pl.BlockSpec(memory_space=pl.ANY)
    return pl.pallas_call(
        functools.partial(_layer_prompt_kernel, nt=nt),
        grid=(n_tiles + 1,),
        in_specs=[_SMEM_SPEC, _const_spec(bias.shape), x_spec, ffn_rows(pe.shape[1])]
                 + [_const_spec(c.shape) for c in consts] + [hbm_spec] * (1 + len(weights)),
        out_specs=[ffn_rows(d), tail(WINDOW, KV_WIDTH), tail(WINDOW, KV_WIDTH),
                   tail(POOL_PAD, POOL_WIDTH), hbm_spec],
        out_shape=[jax.ShapeDtypeStruct((bsz * seq, d), F32),
                   jax.ShapeDtypeStruct((bsz, WINDOW, KV_WIDTH), F32),
                   jax.ShapeDtypeStruct((bsz, WINDOW, KV_WIDTH), F32),
                   jax.ShapeDtypeStruct((bsz, POOL_PAD, POOL_WIDTH), F32),
                   jax.ShapeDtypeStruct((tile, d), F32)],
        scratch_shapes=[pltpu.VMEM((tile, ATTN_WIDTH), BF16),
                        pltpu.VMEM((N_KV_HEADS, kv_rows, LANES), BF16),
                        pltpu.VMEM((N_KV_HEADS, kv_rows, LANES), BF16),
                        pltpu.VMEM((tile + POOL_HALO, POOL_WIDTH), F32),
                        pltpu.VMEM((tile, MIX_WIDTH), BF16),
                        pltpu.VMEM((2, tile, d), F32),
                        pltpu.VMEM((tile, weights[3].shape[1]), BF16)]
                       + [pltpu.VMEM(w.shape, BF16) for w in weights]
                       + [pltpu.VMEM(STAGE_SHAPE, F32), pltpu.SemaphoreType.DMA((STAGE_SHAPE[0],))],
        compiler_params=pltpu.CompilerParams(dimension_semantics=("arbitrary",),
                                             vmem_limit_bytes=VMEM_LIMIT_BYTES),
        name="layer_prompt",
    )(sinks, bias, x, pe, *consts, x1_smp, *weights)


def _mixer_sample(x, ck, cv, st, sinks, bias, g1, win, wpool, pscale, wout, g2):
    nb, lq, d = x.shape
    n_past = ck.shape[1]
    gb = nb // SAMPLE_STEPS
    rows = gb * lq
    group = lambda *tail: pl.BlockSpec((gb,) + tail, lambda i: (i,) + (0,) * len(tail))
    row_blk = lambda w: pl.BlockSpec((rows, w), lambda i: (i, 0))
    kern = functools.partial(_mixer_sample_kernel, nb=gb, lq=lq, past_len=PAST_LEN)
    consts = (g1, win, wpool, pscale, wout, g2)
    return pl.pallas_call(
        kern,
        grid=(SAMPLE_STEPS,),
        in_specs=[_SMEM_SPEC, _const_spec(bias.shape), row_blk(d), group(n_past, KV_WIDTH),
                  group(n_past, KV_WIDTH), group(POOL_PAD, POOL_WIDTH)]
                 + [_const_spec(c.shape) for c in consts],
        out_specs=[row_blk(d), row_blk(KV_WIDTH), row_blk(KV_WIDTH), group(POOL_PAD, POOL_WIDTH)],
        out_shape=[jax.ShapeDtypeStruct((nb * lq, d), F32),
                   jax.ShapeDtypeStruct((nb * lq, KV_WIDTH), F32),
                   jax.ShapeDtypeStruct((nb * lq, KV_WIDTH), F32),
                   jax.ShapeDtypeStruct((nb, POOL_PAD, POOL_WIDTH), F32)],
        scratch_shapes=[pltpu.VMEM((rows, ATTN_WIDTH), BF16),
                        pltpu.VMEM((gb * (POOL_HALO + lq), POOL_WIDTH), F32),
                        pltpu.VMEM((rows, MIX_WIDTH), BF16)],
        compiler_params=pltpu.CompilerParams(dimension_semantics=("arbitrary",),
                                             vmem_limit_bytes=VMEM_LIMIT_BYTES),
        name="mixer_sample",
    )(sinks, bias, x.reshape(nb * lq, d), ck, cv, st, *consts)


def kernel(x_prompt, x_sample, cache_k, cache_v, state_pool, p_prompt, p_sample, rel_bias_table,
           g_mix_pre, w_in, attn_sinks, w_pool, pool_scale, w_out, g_mix_post, g_ffn_pre,
           w_ffn_gate, w_ffn_up, w_ffn_down, g_ffn_post, w_ple, w_ple_gate):
    depth = w_in.shape[0]
    assert depth == 1, "single-layer trunk"
    bsz, seq, d = x_prompt.shape
    nb, lq, _ = x_sample.shape
    n_past = cache_k.shape[2]
    assert seq % PROMPT_TILE == 0 and seq >= WINDOW and PAST_LEN - n_past >= 0
    assert nb * lq == PROMPT_TILE and lq >= POOL_PAD and nb % SAMPLE_STEPS == 0

    i = 0
    row = lambda g: g[i].reshape(1, -1)
    weights = (w_in[i], w_pool[i], w_out[i], w_ffn_gate[i], w_ffn_up[i], w_ffn_down[i], w_ple[i],
               w_ple_gate[i])
    g1, g2, g3, g4 = row(g_mix_pre), row(g_mix_post), row(g_ffn_pre), row(g_ffn_post)
    pscale = row(pool_scale)
    sinks = attn_sinks[i]

    nk = (WIN_CHUNKS + 1) * CHUNK
    bias_prompt = _rel_bias(0, CHUNK, -WIN_CHUNKS * CHUNK, nk, _pad_keys(nk), rel_bias_table)
    bias_sample = _rel_bias(PAST_LEN, lq, PAST_LEN - n_past, n_past + lq, n_past + lq, rel_bias_table)

    ck = cache_k[i].reshape(nb, n_past, KV_WIDTH)
    cv = cache_v[i].reshape(nb, n_past, KV_WIDTH)
    x1s, nks, nvs, nus = _mixer_sample(x_sample, ck, cv, state_pool[i], sinks, bias_sample, g1,
                                       weights[0], weights[1], pscale, weights[2], g2)

    yp, nkp, nvp, nup, ys = _layer_prompt(x_prompt, p_prompt[i].reshape(bsz * seq, PLE_DIM),
                                          p_sample[i].reshape(nb * lq, PLE_DIM), x1s, sinks,
                                          bias_prompt, g1, pscale, g2, g3, g4, weights)
    yp = yp.reshape(bsz, seq, d)
    ys = ys.reshape(nb, lq, d)

    new_k_prompt = nkp.reshape(1, bsz, WINDOW, N_KV_HEADS, HEAD_DIM)
    new_v_prompt = nvp.reshape(1, bsz, WINDOW, N_KV_HEADS, HEAD_DIM)
    new_pool_prompt = nup[None]
    new_k_sample = nks.reshape(1, nb, lq, N_KV_HEADS, HEAD_DIM)
    new_v_sample = nvs.reshape(1, nb, lq, N_KV_HEADS, HEAD_DIM)
    new_pool_sample = nus[None]
    return (yp, ys, new_k_prompt, new_v_prompt, new_pool_prompt, new_k_sample, new_v_sample,
            new_pool_sample)
```

```python
import functools
import math

import jax
import jax.numpy as jnp
from jax import lax
from jax.experimental import pallas as pl
from jax.experimental.pallas import tpu as pltpu

D_MODEL = 1024
CHUNK = 64
HEAD_DIM = 64
N_Q_HEADS = 8
N_KV_HEADS = 2
GQA_GROUP = N_Q_HEADS // N_KV_HEADS
WINDOW = 128
WIN_CHUNKS = WINDOW // CHUNK
ATTN_WIDTH = N_Q_HEADS * HEAD_DIM
KV_WIDTH = N_KV_HEADS * HEAD_DIM
POOL_WINDOWS = (2, 4, 8, 16)
POOL_WIDTH = D_MODEL // 2
POOL_GROUP_WIDTH = POOL_WIDTH // len(POOL_WINDOWS)
POOL_PAD = max(POOL_WINDOWS) - 1
POOL_HALO = POOL_PAD + 1
MIX_WIDTH = ATTN_WIDTH + POOL_WIDTH
IN_WIDTH = ATTN_WIDTH + 2 * KV_WIDTH + POOL_WIDTH
PLE_DIM = 256
N_BUCKETS = 32
MAX_DISTANCE = 128
RMS_EPS = 1e-6
MASK_VALUE = -1e30
PAST_LEN = 1024

LANES = 128
MXU_COLS = 256
LOG2E = math.log2(math.e)
VMEM_LIMIT_BYTES = 58 * 1024 * 1024

PROMPT_TILE = 512
STAGE_SHAPE = (8, 128, 1024)
FF_CHUNK = 256

F32 = jnp.float32
BF16 = jnp.bfloat16


def _rms(x, g):
    ms = jnp.mean(x * x, axis=-1, keepdims=True)
    return x * lax.rsqrt(ms + RMS_EPS) * g


def _dot(a, b):
    return jnp.dot(a, b, preferred_element_type=F32)


def _t5_bucket(rel):
    half = N_BUCKETS // 2
    max_exact = half // 2
    ret = jnp.where(rel > 0, half, 0)
    n = jnp.abs(rel)
    nf = jnp.maximum(n, 1).astype(jnp.float32)
    large = max_exact + (jnp.log(nf / max_exact) / math.log(MAX_DISTANCE / max_exact)
                         * (half - max_exact)).astype(jnp.int32)
    large = jnp.minimum(large, half - 1)
    return ret + jnp.where(n < max_exact, n, large)


def _pad_keys(nk):
    return -(-nk // MXU_COLS) * MXU_COLS


def _lane_low(shape):
    return lax.broadcasted_iota(jnp.int32, shape, len(shape) - 1) < HEAD_DIM


def _rel_bias(q_pos0, lq, k_pos0, lk, lk_pad, table):
    n = lq + lk - 1
    rel = (k_pos0 - (q_pos0 + lq - 1)) + jnp.arange(n, dtype=jnp.int32)
    f = jnp.transpose(table[_t5_bucket(rel)].astype(F32))
    fp = jnp.pad(f, ((0, 0), (0, 1)))
    skew = jnp.tile(fp, (1, lq))[:, :lq * n].reshape(N_Q_HEADS, lq, n)
    bias = skew[:, :, lq - 1:].reshape(N_KV_HEADS, GQA_GROUP * lq, lk) * LOG2E
    return jnp.pad(bias, ((0, 0), (0, 0), (0, lk_pad - lk)), constant_values=MASK_VALUE)


def _sink_col(sink_ref, j, lq):
    blk = lax.broadcasted_iota(jnp.int32, (GQA_GROUP * lq, 1), 0) // lq
    col = jnp.zeros((GQA_GROUP * lq, 1), F32)
    for hl in range(GQA_GROUP):
        col = jnp.where(blk == hl, sink_ref[j * GQA_GROUP + hl] * LOG2E, col)
    return col


def _dup_halves(x):
    xr = pltpu.roll(x, HEAD_DIM, axis=x.ndim - 1)
    lo = _lane_low(x.shape)
    return jnp.where(lo, x, xr), jnp.where(lo, xr, x)


def _stack_queries(q_s, j, nb, lq):
    blocks = []
    for pp in range(2):
        c0 = (2 * j + pp) * LANES
        qp = q_s[:, c0:c0 + LANES].reshape(nb, lq, LANES)
        lo = _lane_low(qp.shape)
        zero = jnp.zeros_like(qp)
        blocks.append(jnp.where(lo, qp, zero))
        blocks.append(jnp.where(lo, zero, qp))
    return jnp.concatenate(blocks, axis=1)


def _softmax(s, sink):
    m = jnp.maximum(jnp.max(s, axis=-1, keepdims=True), sink)
    p = jnp.exp2(s - m)
    den = jnp.sum(p, axis=-1, keepdims=True) + jnp.exp2(sink - m)
    return p.astype(BF16), den


def _softmax_pv(s, sink, vs):
    p, den = _softmax(s, sink)
    return jnp.einsum('bqk,bkd->bqd', p, vs, preferred_element_type=F32) / den


def _unstack_heads(o, lq):
    nb = o.shape[0]
    lo = _lane_low((nb, lq, LANES))
    pairs = []
    for pp in range(2):
        r0 = 2 * pp * lq
        pr = jnp.where(lo, o[:, r0:r0 + lq], o[:, r0 + lq:r0 + 2 * lq])
        pairs.append(pr.reshape(nb * lq, LANES))
    return pairs


def _pool_group(e, g, cnt, halo):
    s = e
    for k in (1, 2, 4, 8)[:g + 1]:
        s = s + pltpu.roll(s, k, axis=0)
    return s[halo:] / cnt - e[halo:]


def _load_weights_bf16(pairs, stage, sem):
    n_slots, srows, scols = stage.shape
    jobs = []
    for src, dst in pairs:
        rows, cols = src.shape
        for r0 in range(0, rows, srows):
            for c0 in range(0, cols, scols):
                jobs.append((src, dst, r0, min(srows, rows - r0), c0, min(scols, cols - c0)))

    def copy(i):
        src, _, r0, nr, c0, ncol = jobs[i]
        slot = i % n_slots
        return pltpu.make_async_copy(src.at[pl.ds(r0, nr), pl.ds(c0, ncol)],
                                     stage.at[slot, pl.ds(0, nr), pl.ds(0, ncol)], sem.at[slot])

    for i in range(min(n_slots - 1, len(jobs))):
        copy(i).start()
    for i, (_, dst, r0, nr, c0, ncol) in enumerate(jobs):
        if i + n_slots - 1 < len(jobs):
            copy(i + n_slots - 1).start()
        copy(i).wait()
        dst[r0:r0 + nr, c0:c0 + ncol] = stage[i % n_slots, :nr, :ncol].astype(BF16)


def _layer_prompt_kernel(sink_ref, bias_ref, x_ref, pe_ref, pe_smp_ref, g1_ref, pscale_ref,
                         g2_ref, g3_ref, g4_ref, x1_smp_hbm,
                         win_hbm, wpool_hbm, wout_hbm, wg_hbm, wu_hbm, wd_hbm, wple_hbm, wpg_hbm,
                         out_ref, nk_ref, nv_ref, npool_ref, out_smp_hbm,
                         q_s, k2_s, v2_s, ue_s, mix_s, x1_s, act_s,
                         win_ref, wpool_ref, wout_ref, wg_ref, wu_ref, wd_ref, wple_ref, wpg_ref,
                         stage_s, sem, *, nt):
    s = pl.program_id(0)
    last = pl.num_programs(0) - 2
    t = lax.rem(jnp.minimum(s, last), nt)
    slot_w = lax.rem(s, 2)
    tile = x_ref.shape[0]
    nc = tile // CHUNK
    nkp = _pad_keys((WIN_CHUNKS + 1) * CHUNK)

    @pl.when(s == 0)
    def _():
        pairs = [(win_hbm, win_ref), (wout_hbm, wout_ref), (wg_hbm, wg_ref), (wu_hbm, wu_ref),
                 (wd_hbm, wd_ref), (wple_hbm, wple_ref), (wpg_hbm, wpg_ref)]
        pairs += [(wpool_hbm.at[g], wpool_ref.at[g]) for g in range(len(POOL_WINDOWS))]
        _load_weights_bf16(pairs, stage_s, sem)
        pltpu.sync_copy(x1_smp_hbm, x1_s.at[1])
        k2_s[...] = jnp.zeros(k2_s.shape, BF16)
        v2_s[...] = jnp.zeros(v2_s.shape, BF16)
        ue_s[...] = jnp.zeros(ue_s.shape, F32)

    def run_step():
        x1 = x1_s[1 - slot_w]
        pe_in = jnp.where(s == 0, pe_smp_ref[...], pe_ref[...])
        d_ff = wg_ref.shape[1]
        ffn = {}
        mixer = {}

        def ffn_norm():
            ffn['h'] = _rms(x1, g3_ref[...]).astype(BF16)

        def ffn_chunk(c):
            c0 = c * FF_CHUNK
            gate = _dot(ffn['h'], wg_ref[:, c0:c0 + FF_CHUNK])
            up = _dot(ffn['h'], wu_ref[:, c0:c0 + FF_CHUNK])
            act_s[:, c0:c0 + FF_CHUNK] = (gate * jax.nn.sigmoid(gate) * up).astype(BF16)

        def ffn_down():
            f = _dot(act_s[...], wd_ref[...])
            ffn['x2'] = x1 + _rms(f, g4_ref[...])

        def ffn_ple():
            x2 = ffn['x2']
            gate2 = jax.nn.sigmoid(_dot(x2.astype(BF16), wpg_ref[...]))
            pe = _dot(pe_in.astype(BF16), wple_ref[...])
            out_ref[...] = x2 + gate2 * pe

        def mixer_in():
            keep = t > 0
            for ref, n_keep in ((k2_s, WINDOW), (v2_s, WINDOW)):
                prev = ref[:, tile:tile + n_keep, :]
                ref[:, :n_keep, :] = jnp.where(keep, prev, jnp.zeros_like(prev))
            prev = ue_s[tile:tile + POOL_HALO, :]
            ue_s[:POOL_HALO, :] = jnp.where(keep, prev, jnp.zeros_like(prev))

            h = _rms(x_ref[...], g1_ref[...]).astype(BF16)
            q_s[...] = (_dot(h, win_ref[:, :ATTN_WIDTH]) * (HEAD_DIM ** -0.5 * LOG2E)).astype(BF16)
            zkv = _dot(h, win_ref[:, ATTN_WIDTH:ATTN_WIDTH + 2 * KV_WIDTH])
            k = zkv[:, :KV_WIDTH]
            v = zkv[:, KV_WIDTH:]
            zu = _dot(h, win_ref[:, ATTN_WIDTH + 2 * KV_WIDTH:])
            ue_s[POOL_HALO:, :] = zu
            nk_ref[...] = k[tile - WINDOW:]
            nv_ref[...] = v[tile - WINDOW:]
            npool_ref[...] = zu[tile - POOL_PAD:]
            for src, dst in ((k, k2_s), (v, v2_s)):
                d0, d1 = _dup_halves(src)
                dst[0, WINDOW:WINDOW + tile, :] = d0.astype(BF16)
                dst[1, WINDOW:WINDOW + tile, :] = d1.astype(BF16)

        def attn_scores(j):
            cidx = lax.broadcasted_iota(jnp.int32, (WIN_CHUNKS, 1, nkp), 0)
            sidx = lax.broadcasted_iota(jnp.int32, (WIN_CHUNKS, 1, nkp), 2)
            valid = ((t * nc + cidx) * CHUNK + sidx - WIN_CHUNKS * CHUNK) >= 0
            qs = _stack_queries(q_s, j, nc, CHUNK)
            ks = jnp.stack([k2_s[j, c * CHUNK:c * CHUNK + nkp, :] for c in range(nc)])
            sc = jnp.einsum('bqd,bkd->bqk', qs, ks, preferred_element_type=F32)
            sc = sc + bias_ref[j][None]
            sc = jnp.concatenate([jnp.where(valid, sc[:WIN_CHUNKS], MASK_VALUE), sc[WIN_CHUNKS:]], axis=0)
            mixer['p', j] = _softmax(sc, _sink_col(sink_ref, j, CHUNK)[None])

        def attn_values(j):
            p, den = mixer.pop(('p', j))
            vs = jnp.stack([v2_s[j, c * CHUNK:c * CHUNK + nkp, :] for c in range(nc)])
            o = jnp.einsum('bqk,bkd->bqd', p, vs, preferred_element_type=F32) / den
            for pp, pr in enumerate(_unstack_heads(o, CHUNK)):
                c0 = (2 * j + pp) * LANES
                mix_s[:, c0:c0 + LANES] = pr.astype(BF16)

        def pool():
            pos = t * tile + lax.broadcasted_iota(jnp.int32, (tile, 1), 0)
            for g, w in enumerate(POOL_WINDOWS):
                c0 = g * POOL_GROUP_WIDTH
                cnt = jnp.minimum(pos + 1, w).astype(F32)
                diff = _pool_group(ue_s[:, c0:c0 + POOL_GROUP_WIDTH], g, cnt, POOL_HALO)
                po = _dot(diff.astype(BF16), wpool_ref[g]) * pscale_ref[:, c0:c0 + POOL_GROUP_WIDTH]
                mix_s[:, ATTN_WIDTH + c0:ATTN_WIDTH + c0 + POOL_GROUP_WIDTH] = po.astype(BF16)

        def mixer_out():
            mix = _dot(mix_s[...], wout_ref[...])
            x1_s[slot_w] = x_ref[...] + _rms(mix, g2_ref[...])

        n_ff = d_ff // FF_CHUNK
        order = [ffn_norm, mixer_in, (ffn_chunk, 0), (ffn_chunk, 1), (attn_scores, 0),
                 (ffn_chunk, 2), (ffn_chunk, 3), (ffn_chunk, 4), (attn_values, 0), (attn_scores, 1),
                 (ffn_chunk, 5), (ffn_chunk, 6), (ffn_chunk, 7), (attn_values, 1), pool]
        order += [(ffn_chunk, c) for c in range(8, n_ff)]
        order += [ffn_down, mixer_out, ffn_ple]
        for step in order:
            fn, args = (step[0], step[1:]) if isinstance(step, tuple) else (step, ())
            fn(*args)

    run_step()

    @pl.when(s == 0)
    def _():
        pltpu.sync_copy(out_ref, out_smp_hbm)


def _mixer_sample_kernel(sink_ref, bias_ref, g1_ref, pscale_ref, g2_ref,
                         x_hbm, ck_hbm, cv_hbm, st_hbm, win_hbm, wpool_hbm, wout_hbm,
                         x1_ref, nk_ref, nv_ref, nu_ref,
                         x_ref, ck_ref, cv_ref, st_ref, win_ref, wpool_ref, wout_ref, sem,
                         q_s, ue_s, mix_s, *, nb, lq, past_len):
    rows = nb * lq
    ext = POOL_HALO + lq
    pairs = dict(x=(x_hbm, x_ref), win=(win_hbm, win_ref), ck=(ck_hbm, ck_ref), cv=(cv_hbm, cv_ref),
                 st=(st_hbm, st_ref), wpool=(wpool_hbm, wpool_ref), wout=(wout_hbm, wout_ref))
    copies = {name: pltpu.make_async_copy(src, dst, sem.at[i])
              for i, (name, (src, dst)) in enumerate(pairs.items())}
    for cp in copies.values():
        cp.start()

    copies['x'].wait()
    x = x_ref[...]
    h = _rms(x, g1_ref[...]).astype(BF16)
    copies['win'].wait()
    win = win_ref[...].astype(BF16)
    q_s[...] = (_dot(h, win[:, :ATTN_WIDTH]) * (HEAD_DIM ** -0.5 * LOG2E)).astype(BF16)
    zkv = _dot(h, win[:, ATTN_WIDTH:ATTN_WIDTH + 2 * KV_WIDTH])
    k = zkv[:, :KV_WIDTH]
    v = zkv[:, KV_WIDTH:]
    zu = _dot(h, win[:, ATTN_WIDTH + 2 * KV_WIDTH:])
    nk_ref[...] = k
    nv_ref[...] = v
    for i in range(nb):
        nu_ref[i] = zu[(i + 1) * lq - POOL_PAD:(i + 1) * lq, :]

    copies['ck'].wait()
    copies['cv'].wait()
    n_past = ck_ref.shape[1]
    kc = _dup_halves(ck_ref[...].reshape(nb * n_past, LANES))
    vc = _dup_halves(cv_ref[...].reshape(nb * n_past, LANES))
    kn = _dup_halves(k)
    vn = _dup_halves(v)
    for j in range(N_KV_HEADS):
        qs = _stack_queries(q_s, j, nb, lq)
        ks = jnp.concatenate([kc[j].astype(BF16).reshape(nb, n_past, LANES),
                              kn[j].astype(BF16).reshape(nb, lq, LANES)], axis=1)
        vs = jnp.concatenate([vc[j].astype(BF16).reshape(nb, n_past, LANES),
                              vn[j].astype(BF16).reshape(nb, lq, LANES)], axis=1)
        s = jnp.einsum('bqd,bkd->bqk', qs, ks, preferred_element_type=F32)
        s = s + bias_ref[j][None]
        o = _softmax_pv(s, _sink_col(sink_ref, j, lq)[None], vs)
        for pp, pr in enumerate(_unstack_heads(o, lq)):
            c0 = (2 * j + pp) * LANES
            mix_s[:, c0:c0 + LANES] = pr.astype(BF16)

    copies['st'].wait()
    copies['wpool'].wait()
    ue_s[...] = jnp.zeros(ue_s.shape, F32)
    for i in range(nb):
        ue_s[i * ext + 1:i * ext + POOL_HALO, :] = st_ref[i]
        ue_s[i * ext + POOL_HALO:(i + 1) * ext, :] = zu[i * lq:(i + 1) * lq, :]
    pos = past_len + lax.broadcasted_iota(jnp.int32, (nb, lq, 1), 1)
    for g, w in enumerate(POOL_WINDOWS):
        c0 = g * POOL_GROUP_WIDTH
        s = ue_s[:, c0:c0 + POOL_GROUP_WIDTH]
        for sh in (1, 2, 4, 8)[:g + 1]:
            s = s + pltpu.roll(s, sh, axis=0)
        s3 = s.reshape(nb, ext, POOL_GROUP_WIDTH)[:, POOL_HALO:, :]
        u3 = zu[:, c0:c0 + POOL_GROUP_WIDTH].reshape(nb, lq, POOL_GROUP_WIDTH)
        cnt = jnp.minimum(pos + 1, w).astype(F32)
        diff = (s3 / cnt - u3).reshape(rows, POOL_GROUP_WIDTH)
        po = _dot(diff.astype(BF16), wpool_ref[g].astype(BF16)) * pscale_ref[:, c0:c0 + POOL_GROUP_WIDTH]
        mix_s[:, ATTN_WIDTH + c0:ATTN_WIDTH + c0 + POOL_GROUP_WIDTH] = po.astype(BF16)

    copies['wout'].wait()
    mix = _dot(mix_s[...], wout_ref[...].astype(BF16))
    x1_ref[...] = x_ref[...] + _rms(mix, g2_ref[...])


def _const_spec(shape):
    nd = len(shape)
    return pl.BlockSpec(shape, lambda *_: (0,) * nd, pipeline_mode=pl.Buffered(1))


_SMEM_SPEC = pl.BlockSpec(memory_space=pltpu.SMEM)


def _layer_prompt(x, pe, pe_smp, x1_smp, sinks, bias, g1, pscale, g2, g3, g4, weights):
    bsz, seq, d = x.shape
    tile = PROMPT_TILE
    nt = seq // tile
    n_tiles = bsz * nt
    nk = (WIN_CHUNKS + 1) * CHUNK
    kv_rows = WINDOW + tile + _pad_keys(nk) - nk
    assert x1_smp.shape == (tile, d) and pe_smp.shape == (tile, pe.shape[1])
    mixer_tile = lambda s: jnp.minimum(s, n_tiles - 1)
    ffn_tile = lambda s: jnp.maximum(s - 1, 0)
    x_spec = pl.BlockSpec((None, tile, d), lambda s: (mixer_tile(s) // nt, mixer_tile(s) % nt, 0))
    tail = lambda r, w: pl.BlockSpec((None, r, w), lambda s: (mixer_tile(s) // nt, 0, 0))
    ffn_rows = lambda w: pl.BlockSpec((tile, w), lambda s: (ffn_tile(s), 0))
    consts = (pe_smp, g1, pscale, g2, g3, g4)
    hbm_spec = pl.BlockSpec(memory_space=pl.ANY)
    return pl.pallas_call(
        functools.partial(_layer_prompt_kernel, nt=nt),
        grid=(n_tiles + 1,),
        in_specs=[_SMEM_SPEC, _const_spec(bias.shape), x_spec, ffn_rows(pe.shape[1])]
                 + [_const_spec(c.shape) for c in consts] + [hbm_spec] * (1 + len(weights)),
        out_specs=[ffn_rows(d), tail(WINDOW, KV_WIDTH), tail(WINDOW, KV_WIDTH),
                   tail(POOL_PAD, POOL_WIDTH), hbm_spec],
        out_shape=[jax.ShapeDtypeStruct((bsz * seq, d), F32),
                   jax.ShapeDtypeStruct((bsz, WINDOW, KV_WIDTH), F32),
                   jax.ShapeDtypeStruct((bsz, WINDOW, KV_WIDTH), F32),
                   jax.ShapeDtypeStruct((bsz, POOL_PAD, POOL_WIDTH), F32),
                   jax.ShapeDtypeStruct((tile, d), F32)],
        scratch_shapes=[pltpu.VMEM((tile, ATTN_WIDTH), BF16),
                        pltpu.VMEM((N_KV_HEADS, kv_rows, LANES), BF16),
                        pltpu.VMEM((N_KV_HEADS, kv_rows, LANES), BF16),
                        pltpu.VMEM((tile + POOL_HALO, POOL_WIDTH), F32),
                        pltpu.VMEM((tile, MIX_WIDTH), BF16),
                        pltpu.VMEM((2, tile, d), F32),
                        pltpu.VMEM((tile, weights[3].shape[1]), BF16)]
                       + [pltpu.VMEM(w.shape, BF16) for w in weights]
                       + [pltpu.VMEM(STAGE_SHAPE, F32), pltpu.SemaphoreType.DMA((STAGE_SHAPE[0],))],
        compiler_params=pltpu.CompilerParams(dimension_semantics=("arbitrary",),
                                             vmem_limit_bytes=VMEM_LIMIT_BYTES),
        name="layer_prompt",
    )(sinks, bias, x, pe, *consts, x1_smp, *weights)


def _mixer_sample(x, ck, cv, st, sinks, bias, g1, win, wpool, pscale, wout, g2):
    nb, lq, d = x.shape
    rows = nb * lq
    full = lambda shape: pl.BlockSpec(shape, lambda i: (0,) * len(shape))
    hbm_spec = pl.BlockSpec(memory_space=pl.ANY)
    kern = functools.partial(_mixer_sample_kernel, nb=nb, lq=lq, past_len=PAST_LEN)
    small = (bias, g1, pscale, g2)
    streamed = (x.reshape(rows, d), ck, cv, st, win, wpool, wout)
    return pl.pallas_call(
        kern,
        grid=(1,),
        in_specs=[_SMEM_SPEC] + [full(a.shape) for a in small] + [hbm_spec] * len(streamed),
        out_specs=[full((rows, d)), full((rows, KV_WIDTH)), full((rows, KV_WIDTH)),
                   full((nb, POOL_PAD, POOL_WIDTH))],
        out_shape=[jax.ShapeDtypeStruct((rows, d), F32),
                   jax.ShapeDtypeStruct((rows, KV_WIDTH), F32),
                   jax.ShapeDtypeStruct((rows, KV_WIDTH), F32),
                   jax.ShapeDtypeStruct((nb, POOL_PAD, POOL_WIDTH), F32)],
        scratch_shapes=[pltpu.VMEM(a.shape, F32) for a in streamed]
                       + [pltpu.SemaphoreType.DMA((len(streamed),)),
                          pltpu.VMEM((rows, ATTN_WIDTH), BF16),
                          pltpu.VMEM((nb * (POOL_HALO + lq), POOL_WIDTH), F32),
                          pltpu.VMEM((rows, MIX_WIDTH), BF16)],
        compiler_params=pltpu.CompilerParams(dimension_semantics=("arbitrary",),
                                             vmem_limit_bytes=VMEM_LIMIT_BYTES),
        name="mixer_sample",
    )(sinks, *small, *streamed)


def kernel(x_prompt, x_sample, cache_k, cache_v, state_pool, p_prompt, p_sample, rel_bias_table,
           g_mix_pre, w_in, attn_sinks, w_pool, pool_scale, w_out, g_mix_post, g_ffn_pre,
           w_ffn_gate, w_ffn_up, w_ffn_down, g_ffn_post, w_ple, w_ple_gate):
    depth = w_in.shape[0]
    assert depth == 1, "single-layer trunk"
    bsz, seq, d = x_prompt.shape
    nb, lq, _ = x_sample.shape
    n_past = cache_k.shape[2]
    assert seq % PROMPT_TILE == 0 and seq >= WINDOW and PAST_LEN - n_past >= 0
    assert nb * lq == PROMPT_TILE and lq >= POOL_PAD

    i = 0
    row = lambda g: g[i].reshape(1, -1)
    weights = (w_in[i], w_pool[i], w_out[i], w_ffn_gate[i], w_ffn_up[i], w_ffn_down[i], w_ple[i],
               w_ple_gate[i])
    g1, g2, g3, g4 = row(g_mix_pre), row(g_mix_post), row(g_ffn_pre), row(g_ffn_post)
    pscale = row(pool_scale)
    sinks = attn_sinks[i]

    nk = (WIN_CHUNKS + 1) * CHUNK
    bias_prompt = _rel_bias(0, CHUNK, -WIN_CHUNKS * CHUNK, nk, _pad_keys(nk), rel_bias_table)
    bias_sample = _rel_bias(PAST_LEN, lq, PAST_LEN - n_past, n_past + lq, n_past + lq, rel_bias_table)

    ck = cache_k[i].reshape(nb, n_past, KV_WIDTH)
    cv = cache_v[i].reshape(nb, n_past, KV_WIDTH)
    x1s, nks, nvs, nus = _mixer_sample(x_sample, ck, cv, state_pool[i], sinks, bias_sample, g1,
                                       weights[0], weights[1], pscale, weights[2], g2)

    yp, nkp, nvp, nup, ys = _layer_prompt(x_prompt, p_prompt[i].reshape(bsz * seq, PLE_DIM),
                                          p_sample[i].reshape(nb * lq, PLE_DIM), x1s, sinks,
                                          bias_prompt, g1, pscale, g2, g3, g4, weights)
    yp = yp.reshape(bsz, seq, d)
    ys = ys.reshape(nb, lq, d)

    new_k_prompt = nkp.reshape(1, bsz, WINDOW, N_KV_HEADS, HEAD_DIM)
    new_v_prompt = nvp.reshape(1, bsz, WINDOW, N_KV_HEADS, HEAD_DIM)
    new_pool_prompt = nup[None]
    new_k_sample = nks.reshape(1, nb, lq, N_KV_HEADS, HEAD_DIM)
    new_v_sample = nvs.reshape(1, nb, lq, N_KV_HEADS, HEAD_DIM)
    new_pool_sample = nus[None]
    return (yp, ys, new_k_prompt, new_v_prompt, new_pool_prompt, new_k_sample, new_v_sample,
            new_pool_sample)
```

```python
import functools
import math

import jax
import jax.numpy as jnp
from jax import lax
from jax.experimental import pallas as pl
from jax.experimental.pallas import tpu as pltpu

D_MODEL = 1024
CHUNK = 64
HEAD_DIM = 64
N_Q_HEADS = 8
N_KV_HEADS = 2
GQA_GROUP = N_Q_HEADS // N_KV_HEADS
WINDOW = 128
WIN_CHUNKS = WINDOW // CHUNK
ATTN_WIDTH = N_Q_HEADS * HEAD_DIM
KV_WIDTH = N_KV_HEADS * HEAD_DIM
POOL_WINDOWS = (2, 4, 8, 16)
POOL_WIDTH = D_MODEL // 2
POOL_GROUP_WIDTH = POOL_WIDTH // len(POOL_WINDOWS)
POOL_PAD = max(POOL_WINDOWS) - 1
POOL_HALO = POOL_PAD + 1
MIX_WIDTH = ATTN_WIDTH + POOL_WIDTH
IN_WIDTH = ATTN_WIDTH + 2 * KV_WIDTH + POOL_WIDTH
PLE_DIM = 256
N_BUCKETS = 32
MAX_DISTANCE = 128
RMS_EPS = 1e-6
MASK_VALUE = -1e30
PAST_LEN = 1024

LANES = 128
MXU_COLS = 256
LOG2E = math.log2(math.e)
VMEM_LIMIT_BYTES = 58 * 1024 * 1024

PROMPT_TILE = 512
STAGE_SHAPE = (8, 128, 1024)
FF_CHUNK = 256

F32 = jnp.float32
BF16 = jnp.bfloat16


def _rms(x, g):
    ms = jnp.mean(x * x, axis=-1, keepdims=True)
    return x * lax.rsqrt(ms + RMS_EPS) * g


def _dot(a, b):
    return jnp.dot(a, b, preferred_element_type=F32)


def _t5_bucket(rel):
    half = N_BUCKETS // 2
    max_exact = half // 2
    ret = jnp.where(rel > 0, half, 0)
    n = jnp.abs(rel)
    nf = jnp.maximum(n, 1).astype(jnp.float32)
    large = max_exact + (jnp.log(nf / max_exact) / math.log(MAX_DISTANCE / max_exact)
                         * (half - max_exact)).astype(jnp.int32)
    large = jnp.minimum(large, half - 1)
    return ret + jnp.where(n < max_exact, n, large)


def _pad_keys(nk):
    return -(-nk // MXU_COLS) * MXU_COLS


def _lane_low(shape):
    return lax.broadcasted_iota(jnp.int32, shape, len(shape) - 1) < HEAD_DIM


def _rel_bias(q_pos0, lq, k_pos0, lk, lk_pad, table):
    n = lq + lk - 1
    rel = (k_pos0 - (q_pos0 + lq - 1)) + jnp.arange(n, dtype=jnp.int32)
    f = jnp.transpose(table[_t5_bucket(rel)].astype(F32))
    fp = jnp.pad(f, ((0, 0), (0, 1)))
    skew = jnp.tile(fp, (1, lq))[:, :lq * n].reshape(N_Q_HEADS, lq, n)
    bias = skew[:, :, lq - 1:].reshape(N_KV_HEADS, GQA_GROUP * lq, lk) * LOG2E
    return jnp.pad(bias, ((0, 0), (0, 0), (0, lk_pad - lk)), constant_values=MASK_VALUE)


def _sink_col(sink_ref, j, lq):
    blk = lax.broadcasted_iota(jnp.int32, (GQA_GROUP * lq, 1), 0) // lq
    col = jnp.zeros((GQA_GROUP * lq, 1), F32)
    for hl in range(GQA_GROUP):
        col = jnp.where(blk == hl, sink_ref[j * GQA_GROUP + hl] * LOG2E, col)
    return col


def _dup_halves(x):
    xr = pltpu.roll(x, HEAD_DIM, axis=x.ndim - 1)
    lo = _lane_low(x.shape)
    return jnp.where(lo, x, xr), jnp.where(lo, xr, x)


def _stack_queries(q_s, j, nb, lq):
    blocks = []
    for pp in range(2):
        c0 = (2 * j + pp) * LANES
        qp = q_s[:, c0:c0 + LANES].reshape(nb, lq, LANES)
        lo = _lane_low(qp.shape)
        zero = jnp.zeros_like(qp)
        blocks.append(jnp.where(lo, qp, zero))
        blocks.append(jnp.where(lo, zero, qp))
    return jnp.concatenate(blocks, axis=1)


def _softmax(s, sink):
    m = jnp.maximum(jnp.max(s, axis=-1, keepdims=True), sink)
    p = jnp.exp2(s - m)
    den = jnp.sum(p, axis=-1, keepdims=True) + jnp.exp2(sink - m)
    return p.astype(BF16), den


def _softmax_pv(s, sink, vs):
    p, den = _softmax(s, sink)
    return jnp.einsum('bqk,bkd->bqd', p, vs, preferred_element_type=F32) / den


def _unstack_heads(o, lq):
    nb = o.shape[0]
    lo = _lane_low((nb, lq, LANES))
    pairs = []
    for pp in range(2):
        r0 = 2 * pp * lq
        pr = jnp.where(lo, o[:, r0:r0 + lq], o[:, r0 + lq:r0 + 2 * lq])
        pairs.append(pr.reshape(nb * lq, LANES))
    return pairs


def _pool_group(e, g, cnt, halo):
    s = e
    for k in (1, 2, 4, 8)[:g + 1]:
        s = s + pltpu.roll(s, k, axis=0)
    return s[halo:] / cnt - e[halo:]


def _load_weights_bf16(pairs, stage, sem):
    n_slots, srows, scols = stage.shape
    jobs = []
    for src, dst in pairs:
        rows, cols = src.shape
        for r0 in range(0, rows, srows):
            for c0 in range(0, cols, scols):
                jobs.append((src, dst, r0, min(srows, rows - r0), c0, min(scols, cols - c0)))

    def copy(i):
        src, _, r0, nr, c0, ncol = jobs[i]
        slot = i % n_slots
        return pltpu.make_async_copy(src.at[pl.ds(r0, nr), pl.ds(c0, ncol)],
                                     stage.at[slot, pl.ds(0, nr), pl.ds(0, ncol)], sem.at[slot])

    for i in range(min(n_slots - 1, len(jobs))):
        copy(i).start()
    for i, (_, dst, r0, nr, c0, ncol) in enumerate(jobs):
        if i + n_slots - 1 < len(jobs):
            copy(i + n_slots - 1).start()
        copy(i).wait()
        dst[r0:r0 + nr, c0:c0 + ncol] = stage[i % n_slots, :nr, :ncol].astype(BF16)


def _layer_prompt_kernel(sink_ref, bias_ref, x_ref, pe_ref, pe_smp_ref, g1_ref, pscale_ref,
                         g2_ref, g3_ref, g4_ref, x1_smp_hbm,
                         win_hbm, wpool_hbm, wout_hbm, wg_hbm, wu_hbm, wd_hbm, wple_hbm, wpg_hbm,
                         out_ref, nk_ref, nv_ref, npool_ref, out_smp_hbm,
                         q_s, k2_s, v2_s, ue_s, mix_s, x1_s, act_s,
                         win_ref, wpool_ref, wout_ref, wg_ref, wu_ref, wd_ref, wple_ref, wpg_ref,
                         stage_s, sem, *, nt):
    s = pl.program_id(0)
    last = pl.num_programs(0) - 2
    t = lax.rem(jnp.minimum(s, last), nt)
    slot_w = lax.rem(s, 2)
    tile = x_ref.shape[0]
    nc = tile // CHUNK
    nkp = _pad_keys((WIN_CHUNKS + 1) * CHUNK)

    @pl.when(s == 0)
    def _():
        pairs = [(win_hbm, win_ref), (wout_hbm, wout_ref), (wg_hbm, wg_ref), (wu_hbm, wu_ref),
                 (wd_hbm, wd_ref), (wple_hbm, wple_ref), (wpg_hbm, wpg_ref)]
        pairs += [(wpool_hbm.at[g], wpool_ref.at[g]) for g in range(len(POOL_WINDOWS))]
        _load_weights_bf16(pairs, stage_s, sem)
        pltpu.sync_copy(x1_smp_hbm, x1_s.at[1])
        k2_s[...] = jnp.zeros(k2_s.shape, BF16)
        v2_s[...] = jnp.zeros(v2_s.shape, BF16)
        ue_s[...] = jnp.zeros(ue_s.shape, F32)

    x1 = x1_s[1 - slot_w]
    pe_in = jnp.where(s == 0, pe_smp_ref[...], pe_ref[...])
    d_ff = wg_ref.shape[1]
    ffn = {}
    mixer = {}

    def ffn_norm():
        ffn['h'] = _rms(x1, g3_ref[...]).astype(BF16)

    def ffn_chunk(c):
        c0 = c * FF_CHUNK
        gate = _dot(ffn['h'], wg_ref[:, c0:c0 + FF_CHUNK])
        up = _dot(ffn['h'], wu_ref[:, c0:c0 + FF_CHUNK])
        act_s[:, c0:c0 + FF_CHUNK] = (gate * jax.nn.sigmoid(gate) * up).astype(BF16)

    def ffn_down():
        f = _dot(act_s[...], wd_ref[...])
        ffn['x2'] = x1 + _rms(f, g4_ref[...])

    def ffn_ple():
        x2 = ffn['x2']
        gate2 = jax.nn.sigmoid(_dot(x2.astype(BF16), wpg_ref[...]))
        pe = _dot(pe_in.astype(BF16), wple_ref[...])
        out_ref[...] = x2 + gate2 * pe

    def mixer_in():
        keep = t > 0
        for ref, n_keep in ((k2_s, WINDOW), (v2_s, WINDOW)):
            prev = ref[:, tile:tile + n_keep, :]
            ref[:, :n_keep, :] = jnp.where(keep, prev, jnp.zeros_like(prev))
        prev = ue_s[tile:tile + POOL_HALO, :]
        ue_s[:POOL_HALO, :] = jnp.where(keep, prev, jnp.zeros_like(prev))

        h = _rms(x_ref[...], g1_ref[...]).astype(BF16)
        q_s[...] = (_dot(h, win_ref[:, :ATTN_WIDTH]) * (HEAD_DIM ** -0.5 * LOG2E)).astype(BF16)
        zkv = _dot(h, win_ref[:, ATTN_WIDTH:ATTN_WIDTH + 2 * KV_WIDTH])
        k = zkv[:, :KV_WIDTH]
        v = zkv[:, KV_WIDTH:]
        zu = _dot(h, win_ref[:, ATTN_WIDTH + 2 * KV_WIDTH:])
        ue_s[POOL_HALO:, :] = zu
        nk_ref[...] = k[tile - WINDOW:]
        nv_ref[...] = v[tile - WINDOW:]
        npool_ref[...] = zu[tile - POOL_PAD:]
        for src, dst in ((k, k2_s), (v, v2_s)):
            d0, d1 = _dup_halves(src.astype(BF16))
            dst[0, WINDOW:WINDOW + tile, :] = d0
            dst[1, WINDOW:WINDOW + tile, :] = d1

    def attn_scores(j):
        cidx = lax.broadcasted_iota(jnp.int32, (WIN_CHUNKS, 1, nkp), 0)
        sidx = lax.broadcasted_iota(jnp.int32, (WIN_CHUNKS, 1, nkp), 2)
        valid = ((t * nc + cidx) * CHUNK + sidx - WIN_CHUNKS * CHUNK) >= 0
        qs = _stack_queries(q_s, j, nc, CHUNK)
        ks = jnp.stack([k2_s[j, c * CHUNK:c * CHUNK + nkp, :] for c in range(nc)])
        sc = jnp.einsum('bqd,bkd->bqk', qs, ks, preferred_element_type=F32)
        sc = sc + bias_ref[j][None]
        sc = jnp.concatenate([jnp.where(valid, sc[:WIN_CHUNKS], MASK_VALUE), sc[WIN_CHUNKS:]], axis=0)
        mixer['p', j] = _softmax(sc, _sink_col(sink_ref, j, CHUNK)[None])

    def attn_values(j):
        p, den = mixer.pop(('p', j))
        vs = jnp.stack([v2_s[j, c * CHUNK:c * CHUNK + nkp, :] for c in range(nc)])
        o = jnp.einsum('bqk,bkd->bqd', p, vs, preferred_element_type=F32) / den
        for pp, pr in enumerate(_unstack_heads(o, CHUNK)):
            c0 = (2 * j + pp) * LANES
            mix_s[:, c0:c0 + LANES] = pr.astype(BF16)

    def pool():
        pos = t * tile + lax.broadcasted_iota(jnp.int32, (tile, 1), 0)
        for g, w in enumerate(POOL_WINDOWS):
            c0 = g * POOL_GROUP_WIDTH
            cnt = jnp.minimum(pos + 1, w).astype(F32)
            diff = _pool_group(ue_s[:, c0:c0 + POOL_GROUP_WIDTH], g, cnt, POOL_HALO)
            po = _dot(diff.astype(BF16), wpool_ref[g]) * pscale_ref[:, c0:c0 + POOL_GROUP_WIDTH]
            mix_s[:, ATTN_WIDTH + c0:ATTN_WIDTH + c0 + POOL_GROUP_WIDTH] = po.astype(BF16)

    def mixer_out():
        mix = _dot(mix_s[...], wout_ref[...])
        x1_s[slot_w] = x_ref[...] + _rms(mix, g2_ref[...])

    n_ff = d_ff // FF_CHUNK
    order = [ffn_norm, mixer_in, (ffn_chunk, 0), (ffn_chunk, 1), (attn_scores, 0),
             (ffn_chunk, 2), (ffn_chunk, 3), (ffn_chunk, 4), (attn_values, 0), (attn_scores, 1),
             (ffn_chunk, 5), (ffn_chunk, 6), (ffn_chunk, 7), (attn_values, 1), pool]
    order += [(ffn_chunk, c) for c in range(8, n_ff)]
    order += [ffn_down, mixer_out, ffn_ple]
    for step in order:
        fn, args = (step[0], step[1:]) if isinstance(step, tuple) else (step, ())
        fn(*args)

    @pl.when(s == 0)
    def _():
        pltpu.sync_copy(out_ref, out_smp_hbm)


def _mixer_sample_kernel(sink_ref, bias_ref, g1_ref, pscale_ref, g2_ref,
                         x_hbm, ck_hbm, cv_hbm, st_hbm, win_hbm, wpool_hbm, wout_hbm,
                         x1_ref, nk_ref, nv_ref, nu_ref,
                         x_ref, ck_ref, cv_ref, st_ref, win_ref, wpool_ref, wout_ref, sem,
                         q_s, ue_s, mix_s, *, nb, lq, past_len):
    rows = nb * lq
    ext = POOL_HALO + lq
    pairs = dict(x=(x_hbm, x_ref), win=(win_hbm, win_ref), ck=(ck_hbm, ck_ref), cv=(cv_hbm, cv_ref),
                 st=(st_hbm, st_ref), wpool=(wpool_hbm, wpool_ref), wout=(wout_hbm, wout_ref))
    copies = {name: pltpu.make_async_copy(src, dst, sem.at[i])
              for i, (name, (src, dst)) in enumerate(pairs.items())}
    for cp in copies.values():
        cp.start()

    copies['x'].wait()
    x = x_ref[...]
    h = _rms(x, g1_ref[...]).astype(BF16)
    copies['win'].wait()
    win = win_ref[...].astype(BF16)
    q_s[...] = (_dot(h, win[:, :ATTN_WIDTH]) * (HEAD_DIM ** -0.5 * LOG2E)).astype(BF16)
    zkv = _dot(h, win[:, ATTN_WIDTH:ATTN_WIDTH + 2 * KV_WIDTH])
    k = zkv[:, :KV_WIDTH]
    v = zkv[:, KV_WIDTH:]
    zu = _dot(h, win[:, ATTN_WIDTH + 2 * KV_WIDTH:])
    nk_ref[...] = k
    nv_ref[...] = v
    for i in range(nb):
        nu_ref[i] = zu[(i + 1) * lq - POOL_PAD:(i + 1) * lq, :]

    copies['ck'].wait()
    copies['cv'].wait()
    n_past = ck_ref.shape[1]
    kc = _dup_halves(ck_ref[...].astype(BF16).reshape(nb * n_past, LANES))
    vc = _dup_halves(cv_ref[...].astype(BF16).reshape(nb * n_past, LANES))
    kn = _dup_halves(k.astype(BF16))
    vn = _dup_halves(v.astype(BF16))
    for j in range(N_KV_HEADS):
        qs = _stack_queries(q_s, j, nb, lq)
        ks = jnp.concatenate([kc[j].reshape(nb, n_past, LANES), kn[j].reshape(nb, lq, LANES)], axis=1)
        vs = jnp.concatenate([vc[j].reshape(nb, n_past, LANES), vn[j].reshape(nb, lq, LANES)], axis=1)
        s = jnp.einsum('bqd,bkd->bqk', qs, ks, preferred_element_type=F32)
        s = s + bias_ref[j][None]
        o = _softmax_pv(s, _sink_col(sink_ref, j, lq)[None], vs)
        for pp, pr in enumerate(_unstack_heads(o, lq)):
            c0 = (2 * j + pp) * LANES
            mix_s[:, c0:c0 + LANES] = pr.astype(BF16)

    copies['st'].wait()
    copies['wpool'].wait()
    ue_s[...] = jnp.zeros(ue_s.shape, F32)
    for i in range(nb):
        ue_s[i * ext + 1:i * ext + POOL_HALO, :] = st_ref[i]
        ue_s[i * ext + POOL_HALO:(i + 1) * ext, :] = zu[i * lq:(i + 1) * lq, :]
    pos = past_len + lax.broadcasted_iota(jnp.int32, (nb, lq, 1), 1)
    for g, w in enumerate(POOL_WINDOWS):
        c0 = g * POOL_GROUP_WIDTH
        s = ue_s[:, c0:c0 + POOL_GROUP_WIDTH]
        for sh in (1, 2, 4, 8)[:g + 1]:
            s = s + pltpu.roll(s, sh, axis=0)
        s3 = s.reshape(nb, ext, POOL_GROUP_WIDTH)[:, POOL_HALO:, :]
        u3 = zu[:, c0:c0 + POOL_GROUP_WIDTH].reshape(nb, lq, POOL_GROUP_WIDTH)
        cnt = jnp.minimum(pos + 1, w).astype(F32)
        diff = (s3 / cnt - u3).reshape(rows, POOL_GROUP_WIDTH)
        po = _dot(diff.astype(BF16), wpool_ref[g].astype(BF16)) * pscale_ref[:, c0:c0 + POOL_GROUP_WIDTH]
        mix_s[:, ATTN_WIDTH + c0:ATTN_WIDTH + c0 + POOL_GROUP_WIDTH] = po.astype(BF16)

    copies['wout'].wait()
    mix = _dot(mix_s[...], wout_ref[...].astype(BF16))
    x1_ref[...] = x_ref[...] + _rms(mix, g2_ref[...])


def _const_spec(shape):
    nd = len(shape)
    return pl.BlockSpec(shape, lambda *_: (0,) * nd, pipeline_mode=pl.Buffered(1))


_SMEM_SPEC = pl.BlockSpec(memory_space=pltpu.SMEM)


def _layer_prompt(x, pe, pe_smp, x1_smp, sinks, bias, g1, pscale, g2, g3, g4, weights):
    bsz, seq, d = x.shape
    tile = PROMPT_TILE
    nt = seq // tile
    n_tiles = bsz * nt
    nk = (WIN_CHUNKS + 1) * CHUNK
    kv_rows = WINDOW + tile + _pad_keys(nk) - nk
    assert x1_smp.shape == (tile, d) and pe_smp.shape == (tile, pe.shape[1])
    mixer_tile = lambda s: jnp.minimum(s, n_tiles - 1)
    ffn_tile = lambda s: jnp.maximum(s - 1, 0)
    x_spec = pl.BlockSpec((None, tile, d), lambda s: (mixer_tile(s) // nt, mixer_tile(s) % nt, 0))
    tail = lambda r, w: pl.BlockSpec((None, r, w), lambda s: (mixer_tile(s) // nt, 0, 0))
    ffn_rows = lambda w: pl.BlockSpec((tile, w), lambda s: (ffn_tile(s), 0))
    consts = (pe_smp, g1, pscale, g2, g3, g4)
    hbm_spec = pl.BlockSpec(memory_space=pl.ANY)
    return pl.pallas_call(
        functools.partial(_layer_prompt_kernel, nt=nt),
        grid=(n_tiles + 1,),
        in_specs=[_SMEM_SPEC, _const_spec(bias.shape), x_spec, ffn_rows(pe.shape[1])]
                 + [_const_spec(c.shape) for c in consts] + [hbm_spec] * (1 + len(weights)),
        out_specs=[ffn_rows(d), tail(WINDOW, KV_WIDTH), tail(WINDOW, KV_WIDTH),
                   tail(POOL_PAD, POOL_WIDTH), hbm_spec],
        out_shape=[jax.ShapeDtypeStruct((bsz * seq, d), F32),
                   jax.ShapeDtypeStruct((bsz, WINDOW, KV_WIDTH), F32),
                   jax.ShapeDtypeStruct((bsz, WINDOW, KV_WIDTH), F32),
                   jax.ShapeDtypeStruct((bsz, POOL_PAD, POOL_WIDTH), F32),
                   jax.ShapeDtypeStruct((tile, d), F32)],
        scratch_shapes=[pltpu.VMEM((tile, ATTN_WIDTH), BF16),
                        pltpu.VMEM((N_KV_HEADS, kv_rows, LANES), BF16),
                        pltpu.VMEM((N_KV_HEADS, kv_rows, LANES), BF16),
                        pltpu.VMEM((tile + POOL_HALO, POOL_WIDTH), F32),
                        pltpu.VMEM((tile, MIX_WIDTH), BF16),
                        pltpu.VMEM((2, tile, d), F32),
                        pltpu.VMEM((tile, weights[3].shape[1]), BF16)]
                       + [pltpu.VMEM(w.shape, BF16) for w in weights]
                       + [pltpu.VMEM(STAGE_SHAPE, F32), pltpu.SemaphoreType.DMA((STAGE_SHAPE[0],))],
        compiler_params=pltpu.CompilerParams(dimension_semantics=("arbitrary",),
                                             vmem_limit_bytes=VMEM_LIMIT_BYTES),
        name="layer_prompt",
    )(sinks, bias, x, pe, *consts, x1_smp, *weights)


def _mixer_sample(x, ck, cv, st, sinks, bias, g1, win, wpool, pscale, wout, g2):
    nb, lq, d = x.shape
    rows = nb * lq
    full = lambda shape: pl.BlockSpec(shape, lambda i: (0,) * len(shape))
    hbm_spec = pl.BlockSpec(memory_space=pl.ANY)
    kern = functools.partial(_mixer_sample_kernel, nb=nb, lq=lq, past_len=PAST_LEN)
    small = (bias, g1, pscale, g2)
    streamed = (x.reshape(rows, d), ck, cv, st, win, wpool, wout)
    return pl.pallas_call(
        kern,
        grid=(1,),
        in_specs=[_SMEM_SPEC] + [full(a.shape) for a in small] + [hbm_spec] * len(streamed),
        out_specs=[full((rows, d)), full((rows, KV_WIDTH)), full((rows, KV_WIDTH)),
                   full((nb, POOL_PAD, POOL_WIDTH))],
        out_shape=[jax.ShapeDtypeStruct((rows, d), F32),
                   jax.ShapeDtypeStruct((rows, KV_WIDTH), F32),
                   jax.ShapeDtypeStruct((rows, KV_WIDTH), F32),
                   jax.ShapeDtypeStruct((nb, POOL_PAD, POOL_WIDTH), F32)],
        scratch_shapes=[pltpu.VMEM(a.shape, F32) for a in streamed]
                       + [pltpu.SemaphoreType.DMA((len(streamed),)),
                          pltpu.VMEM((rows, ATTN_WIDTH), BF16),
                          pltpu.VMEM((nb * (POOL_HALO + lq), POOL_WIDTH), F32),
                          pltpu.VMEM((rows, MIX_WIDTH), BF16)],
        compiler_params=pltpu.CompilerParams(dimension_semantics=("arbitrary",),
                                             vmem_limit_bytes=VMEM_LIMIT_BYTES),
        name="mixer_sample",
    )(sinks, *small, *streamed)


def kernel(x_prompt, x_sample, cache_k, cache_v, state_pool, p_prompt, p_sample, rel_bias_table,
           g_mix_pre, w_in, attn_sinks, w_pool, pool_scale, w_out, g_mix_post, g_ffn_pre,
           w_ffn_gate, w_ffn_up, w_ffn_down, g_ffn_post, w_ple, w_ple_gate):
    depth = w_in.shape[0]
    assert depth == 1, "single-layer trunk"
    bsz, seq, d = x_prompt.shape
    nb, lq, _ = x_sample.shape
    n_past = cache_k.shape[2]
    assert seq % PROMPT_TILE == 0 and seq >= WINDOW and PAST_LEN - n_past >= 0
    assert nb * lq == PROMPT_TILE and lq >= POOL_PAD

    i = 0
    row = lambda g: g[i].reshape(1, -1)
    weights = (w_in[i], w_pool[i], w_out[i], w_ffn_gate[i], w_ffn_up[i], w_ffn_down[i], w_ple[i],
               w_ple_gate[i])
    g1, g2, g3, g4 = row(g_mix_pre), row(g_mix_post), row(g_ffn_pre), row(g_ffn_post)
    pscale = row(pool_scale)
    sinks = attn_sinks[i]

    nk = (WIN_CHUNKS + 1) * CHUNK
    bias_prompt = _rel_bias(0, CHUNK, -WIN_CHUNKS * CHUNK, nk, _pad_keys(nk), rel_bias_table)
    bias_sample = _rel_bias(PAST_LEN, lq, PAST_LEN - n_past, n_past + lq, n_past + lq, rel_bias_table)

    ck = cache_k[i].reshape(nb, n_past, KV_WIDTH)
    cv = cache_v[i].reshape(nb, n_past, KV_WIDTH)
    x1s, nks, nvs, nus = _mixer_sample(x_sample, ck, cv, state_pool[i], sinks, bias_sample, g1,
                                       weights[0], weights[1], pscale, weights[2], g2)

    yp, nkp, nvp, nup, ys = _layer_prompt(x_prompt, p_prompt[i].reshape(bsz * seq, PLE_DIM),
                                          p_sample[i].reshape(nb * lq, PLE_DIM), x1s, sinks,
                                          bias_prompt, g1, pscale, g2, g3, g4, weights)
    yp = yp.reshape(bsz, seq, d)
    ys = ys.reshape(nb, lq, d)

    new_k_prompt = nkp.reshape(1, bsz, WINDOW, N_KV_HEADS, HEAD_DIM)
    new_v_prompt = nvp.reshape(1, bsz, WINDOW, N_KV_HEADS, HEAD_DIM)
    new_pool_prompt = nup[None]
    new_k_sample = nks.reshape(1, nb, lq, N_KV_HEADS, HEAD_DIM)
    new_v_sample = nvs.reshape(1, nb, lq, N_KV_HEADS, HEAD_DIM)
    new_pool_sample = nus[None]
    return (yp, ys, new_k_prompt, new_v_prompt, new_pool_prompt, new_k_sample, new_v_sample,
            new_pool_sample)
```
